```python
import math
import jax, jax.numpy as jnp
from jax import lax
import numpy as np

D_MODEL = 2048
BATCH = 4
SEQ = 2048
DEPTH = 4
DEC_BATCH = 128
DEC_SEQ = 1
PAST_LEN = 16384
PAGE_SIZE = 128

A_GROUPS = 8
A_GROUP_DIM = D_MODEL // 16
A_WIDTH = A_GROUPS * A_GROUP_DIM
CHUNK = 128
B_HEADS = 8
B_HEAD_K = 128
B_HEAD_V = D_MODEL // 16
B_KW = B_HEADS * B_HEAD_K
B_VW = B_HEADS * B_HEAD_V
HGRN_CHUNK = 64
N_EXPERTS = 16
N_EXPERT_GROUPS = 4
EXPERTS_PER_GROUP = N_EXPERTS // N_EXPERT_GROUPS
TOP_K = 2
D_FF = D_MODEL // 2
ALPHA = (2.0 * DEPTH) ** 0.25
BETA = (8.0 * DEPTH) ** -0.25
LN_EPS = 1e-5
IN_COLS = 2 * A_WIDTH + 2 * B_KW + 2 * B_VW + 2 * D_MODEL

kernel_name = "hybrid_sgu_hgrn2_grouped_moe_deepnorm_step"


def _split_points():
    sizes = [A_WIDTH, A_WIDTH, B_KW, B_KW, B_VW, B_VW, D_MODEL]
    return [int(s) for s in np.cumsum(sizes)]


def layer_norm(x, g, b):
    xf = x.astype(jnp.float32)
    mu = jnp.mean(xf, axis=-1, keepdims=True)
    var = jnp.mean(jnp.square(xf - mu), axis=-1, keepdims=True)
    y = (xf - mu) * lax.rsqrt(var + LN_EPS)
    return (y * g.astype(jnp.float32) + b.astype(jnp.float32)).astype(x.dtype)


def chunk_spatial_gating(u_pre, v_pre, ln_g, ln_b, w_s, b_s):
    B, T, _ = u_pre.shape
    L = min(T, CHUNK)
    u = jax.nn.gelu(u_pre)
    v = layer_norm(jax.nn.gelu(v_pre), ln_g, ln_b)
    causal = jnp.tril(jnp.ones((L, L), dtype=bool))
    w = jnp.where(causal[None], w_s[:, :L, :L], 0.0)
    vc = v.reshape(B, T // L, L, A_GROUPS, A_GROUP_DIM)
    s = jnp.einsum('gts,bnsgc->bntgc', w, vc) + b_s[:, :L].T[None, None, :, :, None]
    return u * s.reshape(B, T, A_WIDTH), v


def _hgrn2_chunk_step(S, inp):
    qc, kc, vc, gc = inp
    C = qc.shape[2]
    causal = jnp.tril(jnp.ones((C, C), dtype=bool))[None, None, :, :, None]
    diff = gc[:, :, :, None, :] - gc[:, :, None, :, :]
    decay = jnp.where(causal, jnp.exp(jnp.where(causal, diff, 0.0)), 0.0)
    scores = jnp.einsum('bhtd,bhsd,bhtsd->bhts', qc, kc, decay)
    o = (jnp.einsum('bhts,bhsv->bhtv', scores, vc)
         + jnp.einsum('bhtd,bhdv->bhtv', qc * jnp.exp(gc), S))
    g_last = gc[:, :, -1:, :]
    S_new = (jnp.exp(g_last[:, :, 0, :])[..., None] * S
             + jnp.einsum('bhsd,bhsv->bhdv', kc * jnp.exp(g_last - gc), vc))
    return S_new, o


def hgrn2_mix(q_pre, f_pre, i_pre, g_pre, lb, norm_g, S0):
    B, T, _ = q_pre.shape
    f32 = jnp.float32
    q = jax.nn.silu(q_pre.astype(f32)).reshape(B, T, B_HEADS, B_HEAD_K)
    z = f_pre.astype(f32).reshape(B, T, B_HEADS, B_HEAD_K)
    lb = lb.reshape(B_HEADS, B_HEAD_K)
    forget = lb + (1.0 - lb) * jax.nn.sigmoid(z)
    logf = jnp.log(forget)
    k = (1.0 - lb) * jax.nn.sigmoid(-z)
    v = i_pre.astype(f32).reshape(B, T, B_HEADS, B_HEAD_V)
    C = math.gcd(T, HGRN_CHUNK)
    N = T // C

    def to_chunks(a):
        return a.reshape(B, N, C, B_HEADS, a.shape[-1]).transpose(1, 0, 3, 2, 4)

    gcum = jnp.cumsum(to_chunks(logf), axis=3)
    S_final, o = lax.scan(_hgrn2_chunk_step, S0.astype(f32),
                          (to_chunks(q), to_chunks(k), to_chunks(v), gcum))
    o = o.transpose(1, 0, 3, 2, 4).reshape(B, T, B_HEADS, B_HEAD_V)
    o = o * lax.rsqrt(jnp.mean(jnp.square(o), axis=-1, keepdims=True) + LN_EPS)
    o = o.reshape(B, T, B_VW) * norm_g.astype(f32) * jax.nn.silu(g_pre.astype(f32))
    return o.astype(q_pre.dtype), S_final.astype(S0.dtype)


def moe_ffn(x, router_w, router_b, w_gate, w_up, w_down):
    B, T, D = x.shape
    f32 = jnp.float32
    xt = x.reshape(B * T, D)
    logits = (xt @ router_w).astype(f32) + router_b.astype(f32)
    probs = jax.nn.softmax(logits, axis=-1)
    grouped = probs.reshape(-1, N_EXPERT_GROUPS, EXPERTS_PER_GROUP)
    group_score = jnp.sum(lax.top_k(grouped, TOP_K)[0], axis=-1)
    sel_group = jnp.argmax(group_score, axis=-1)
    in_group = jnp.take_along_axis(grouped, sel_group[:, None, None], axis=1)[:, 0]
    top_w, top_i = lax.top_k(in_group, TOP_K)
    top_w = top_w / jnp.sum(top_w, axis=-1, keepdims=True)
    expert_ids = sel_group[:, None] * EXPERTS_PER_GROUP + top_i
    combine = jnp.sum(jax.nn.one_hot(expert_ids, N_EXPERTS, dtype=f32) * top_w[..., None], axis=1)
    y = jnp.zeros((B * T, D), f32)
    for e in range(N_EXPERTS):
        h = jax.nn.silu(xt @ w_gate[e]) * (xt @ w_up[e])
        y = y + combine[:, e:e + 1] * (h @ w_down[e]).astype(f32)
    return y.reshape(B, T, D).astype(x.dtype)


def decoder_layer(x, S0, lb, w_in, sgu_ln_g, sgu_ln_b, w_s, b_s, hgrn_norm_g, w_out_a, w_out_b,
                  w_o, ln1_g, ln1_b, router_w, router_b, w_gate, w_up, w_down, ln2_g, ln2_b):
    proj = jnp.einsum('btd,dc->btc', x, w_in)
    u_pre, v_pre, q_pre, f_pre, i_pre, g_pre, gate_a, gate_b = jnp.split(proj, _split_points(), axis=-1)
    y_a, v_rows = chunk_spatial_gating(u_pre, v_pre, sgu_ln_g, sgu_ln_b, w_s, b_s)
    y_b, S_new = hgrn2_mix(q_pre, f_pre, i_pre, g_pre, lb, hgrn_norm_g, S0)
    merged = jax.nn.sigmoid(gate_a) * (y_a @ w_out_a) + jax.nn.sigmoid(gate_b) * (y_b @ w_out_b)
    x = layer_norm(ALPHA * x + merged @ w_o, ln1_g, ln1_b)
    x = layer_norm(ALPHA * x + moe_ffn(x, router_w, router_b, w_gate, w_up, w_down), ln2_g, ln2_b)
    return x, S_new, v_rows


def trunk(x, init_states, lower_bounds, w_in, sgu_ln_g, sgu_ln_b, w_s, b_s, hgrn_norm_g, w_out_a,
          w_out_b, w_o, ln1_g, ln1_b, router_w, router_b, w_gate, w_up, w_down, ln2_g, ln2_b):
    states, v_rows = [], []
    for l in range(DEPTH):
        x, S, v = decoder_layer(x, init_states[l], lower_bounds[l], w_in[l], sgu_ln_g[l], sgu_ln_b[l],
                                w_s[l], b_s[l], hgrn_norm_g[l], w_out_a[l], w_out_b[l], w_o[l],
                                ln1_g[l], ln1_b[l], router_w, router_b, w_gate[l], w_up[l], w_down[l],
                                ln2_g[l], ln2_b[l])
        states.append(S)
        v_rows.append(v)
    return x, states, v_rows


def setup_inputs(seed: int = 0) -> dict:
    key = jax.random.key(seed)
    ks = jax.random.split(key, 24)
    f32 = jnp.float32

    def nrm(k, shape, scale):
        return scale * jax.random.normal(k, shape, f32)

    return {
        "x_prompt": nrm(ks[0], (BATCH, SEQ, D_MODEL), 1.0),
        "x_sample": nrm(ks[1], (DEC_BATCH, DEC_SEQ, D_MODEL), 1.0),
        "state_hgrn": nrm(ks[2], (DEPTH, DEC_BATCH, B_HEADS, B_HEAD_K, B_HEAD_V), 0.5),
        "w_in": nrm(ks[3], (DEPTH, D_MODEL, IN_COLS), D_MODEL ** -0.5),
        "sgu_ln_g": 1.0 + nrm(ks[4], (DEPTH, A_WIDTH), 0.02),
        "sgu_ln_b": nrm(ks[5], (DEPTH, A_WIDTH), 0.02),
        "w_s": nrm(ks[6], (DEPTH, A_GROUPS, CHUNK, CHUNK), CHUNK ** -0.5),
        "b_s": 1.0 + nrm(ks[7], (DEPTH, A_GROUPS, CHUNK), 0.1),
        "lb_logits": nrm(ks[8], (DEPTH, B_KW), 0.5),
        "hgrn_norm_g": 1.0 + nrm(ks[9], (DEPTH, B_VW), 0.02),
        "w_out_a": nrm(ks[10], (DEPTH, A_WIDTH, D_MODEL), BETA * A_WIDTH ** -0.5),
        "w_out_b": nrm(ks[11], (DEPTH, B_VW, D_MODEL), BETA * B_VW ** -0.5),
        "w_o": nrm(ks[12], (DEPTH, D_MODEL, D_MODEL), BETA * D_MODEL ** -0.5),
        "ln1_g": 1.0 + nrm(ks[13], (DEPTH, D_MODEL), 0.02),
        "ln1_b": nrm(ks[14], (DEPTH, D_MODEL), 0.02),
        "router_w": nrm(ks[15], (D_MODEL, N_EXPERTS), D_MODEL ** -0.5),
        "router_b": nrm(ks[16], (N_EXPERTS,), 0.01),
        "w_gate": nrm(ks[17], (DEPTH, N_EXPERTS, D_MODEL, D_FF), D_MODEL ** -0.5),
        "w_up": nrm(ks[18], (DEPTH, N_EXPERTS, D_MODEL, D_FF), D_MODEL ** -0.5),
        "w_down": nrm(ks[19], (DEPTH, N_EXPERTS, D_FF, D_MODEL), BETA * D_FF ** -0.5),
        "ln2_g": 1.0 + nrm(ks[20], (DEPTH, D_MODEL), 0.02),
        "ln2_b": nrm(ks[21], (DEPTH, D_MODEL), 0.02),
    }


def reference(x_prompt, x_sample, state_hgrn, w_in, sgu_ln_g, sgu_ln_b, w_s, b_s, lb_logits,
              hgrn_norm_g, w_out_a, w_out_b, w_o, ln1_g, ln1_b, router_w, router_b, w_gate, w_up,
              w_down, ln2_g, ln2_b):
    p = jax.nn.softmax(lb_logits.astype(jnp.float32), axis=0)
    lower_bounds = jnp.cumsum(p, axis=0) - p[0:1]

    zero_state = jnp.zeros((BATCH, B_HEADS, B_HEAD_K, B_HEAD_V), jnp.float32)
    y_prompt, s_prompt, _ = trunk(x_prompt, [zero_state] * DEPTH, lower_bounds, w_in, sgu_ln_g,
                                  sgu_ln_b, w_s, b_s, hgrn_norm_g, w_out_a, w_out_b, w_o, ln1_g,
                                  ln1_b, router_w, router_b, w_gate, w_up, w_down, ln2_g, ln2_b)
    y_sample, s_sample, v_sample = trunk(x_sample, [state_hgrn[l] for l in range(DEPTH)],
                                         lower_bounds, w_in, sgu_ln_g, sgu_ln_b, w_s, b_s,
                                         hgrn_norm_g, w_out_a, w_out_b, w_o, ln1_g, ln1_b,
                                         router_w, router_b, w_gate, w_up, w_down, ln2_g, ln2_b)
    state_hgrn_prompt = jnp.stack(s_prompt)
    state_hgrn_sample = jnp.stack(s_sample)
    state_chunk_v_sample = jnp.stack(v_sample)
    return (y_prompt, y_sample, state_hgrn_prompt, state_hgrn_sample, state_chunk_v_sample)
```

```python
import functools

import jax
import jax.numpy as jnp
from jax import lax
from jax.experimental import pallas as pl
from jax.experimental.pallas import tpu as pltpu

F32 = jnp.float32
BF16 = jnp.bfloat16

SGU_GROUPS = 8
SGU_CHUNK = 128
HGRN_HEADS = 8
HGRN_DK = 128
HGRN_DV = 128
N_EXPERTS = 16
N_EXPERT_GROUPS = 4
EXPERTS_PER_GROUP = N_EXPERTS // N_EXPERT_GROUPS
LN_EPS = 1e-5

V7X_LANES = 128
V7X_SUBLANES = 8
V7X_VMEM_BYTES = 64 * 1024 * 1024

ROW_TILE = 640
MERGE_ROW_TILE = 320
PROJ_COL_TILE = 1024
HGRN_CHUNK = 128
HGRN_BAND = 8
SAMPLE_SEQ_TILE = 16
MOE_ROW_TILE = 256
COMBINE_ROW_TILE = 640


def _params(vmem_mb, n_axes=1):
    return pltpu.CompilerParams(
        dimension_semantics=("arbitrary",) * n_axes,
        vmem_limit_bytes=vmem_mb * 1024 * 1024,
    )


def _layer_norm(x, g, b):
    mu = jnp.mean(x, axis=-1, keepdims=True)
    xc = x - mu
    var = jnp.mean(xc * xc, axis=-1, keepdims=True)
    return xc * lax.rsqrt(var + LN_EPS) * g + b


def _silu(x):
    return x * jax.nn.sigmoid(x)


def _matmul_kernel(x_ref, w_ref, o_ref):
    o_ref[...] = jnp.dot(x_ref[...], w_ref[...], preferred_element_type=F32).astype(o_ref.dtype)


def _matmul(x, w, *, tm, tn, out_dtype):
    m, k = x.shape
    _, n = w.shape
    return pl.pallas_call(
        _matmul_kernel,
        grid=(n // tn, m // tm),
        in_specs=[
            pl.BlockSpec((tm, k), lambda j, i: (i, 0)),
            pl.BlockSpec((k, tn), lambda j, i: (0, j)),
        ],
        out_specs=pl.BlockSpec((tm, tn), lambda j, i: (i, j)),
        out_shape=jax.ShapeDtypeStruct((m, n), out_dtype),
        compiler_params=_params(40, 2),
        name="proj_matmul",
    )(x, w)


def _sgu_kernel(u_ref, v_ref, w_ref, bias_ref, g_ref, b_ref, ya_ref, vrow_ref, *,
                chunks_per_tile, n_prompt_chunks, n_steps):
    i = pl.program_id(0)
    u = jax.nn.gelu(u_ref[...].astype(F32))
    vn = _layer_norm(jax.nn.gelu(v_ref[...].astype(F32)), g_ref[...], b_ref[...])
    gd = vn.shape[-1] // SGU_GROUPS
    for c in range(chunks_per_tile):
        rows = slice(c * SGU_CHUNK, (c + 1) * SGU_CHUNK)
        variant = (i * chunks_per_tile + c >= n_prompt_chunks).astype(jnp.int32)
        w = w_ref[variant]
        vc = vn[rows].astype(BF16)
        parts = [
            jnp.dot(w[g], vc[:, g * gd:(g + 1) * gd], preferred_element_type=F32)
            for g in range(SGU_GROUPS)
        ]
        s = jnp.concatenate(parts, axis=-1) + bias_ref[variant]
        ya_ref[rows, :] = (u[rows] * s).astype(ya_ref.dtype)

    @pl.when(i == n_steps - 1)
    def _():
        last = slice((chunks_per_tile - 1) * SGU_CHUNK, chunks_per_tile * SGU_CHUNK)
        vrow_ref[...] = vn[last]


def _sgu(proj, w_eff, bias_eff, ln_g, ln_b, *, n_prompt_rows, width):
    nt = proj.shape[0]
    tr = ROW_TILE
    n_steps = nt // tr
    kern = functools.partial(
        _sgu_kernel, chunks_per_tile=tr // SGU_CHUNK,
        n_prompt_chunks=n_prompt_rows // SGU_CHUNK, n_steps=n_steps)
    return pl.pallas_call(
        kern,
        grid=(n_steps,),
        in_specs=[
            pl.BlockSpec((tr, width), lambda i: (i, 0)),
            pl.BlockSpec((tr, width), lambda i: (i, 1)),
            pl.BlockSpec(w_eff.shape, lambda i: (0, 0, 0, 0)),
            pl.BlockSpec(bias_eff.shape, lambda i: (0, 0, 0)),
            pl.BlockSpec((1, width), lambda i: (0, 0)),
            pl.BlockSpec((1, width), lambda i: (0, 0)),
        ],
        out_specs=[
            pl.BlockSpec((tr, width), lambda i: (i, 0)),
            pl.BlockSpec((SGU_CHUNK, width), lambda i: (0, 0)),
        ],
        out_shape=[
            jax.ShapeDtypeStruct((nt, width), BF16),
            jax.ShapeDtypeStruct((SGU_CHUNK, width), F32),
        ],
        compiler_params=_params(40),
        name="sgu",
    )(proj, proj, w_eff, bias_eff, ln_g, ln_b)


def _hgrn_gates(qp, z, lb):
    q = _silu(qp)
    one_m_lb = 1.0 - lb
    forget = lb + one_m_lb * jax.nn.sigmoid(z)
    k = one_m_lb * jax.nn.sigmoid(-z)
    return q, forget, k


def _hgrn_prompt_kernel(q_ref, f_ref, i_ref, g_ref, lb_ref, ng_ref, yin_ref, y_ref, s_ref,
                        st_ref, lvl_ref, *, n_chunks):
    del yin_ref
    c = HGRN_CHUNK
    w = HGRN_BAND
    lb = lb_ref[...]
    ng = ng_ref[...]

    row = lax.broadcasted_iota(jnp.int32, (c, c), 0)
    col = lax.broadcasted_iota(jnp.int32, (c, c), 1)
    tril = jnp.where(row >= col, 1.0, 0.0).astype(F32)
    x = row ^ col
    lvl = jnp.where(x >= 64, 64, jnp.where(x >= 32, 32, jnp.where(x >= 16, 16, jnp.where(x >= 8, 8, 0))))
    lvl_ref[...] = jnp.where(row - col >= w, lvl, 0)
    st_ref[...] = jnp.zeros_like(st_ref)
    ones = jnp.ones((HGRN_DK, V7X_LANES), BF16)
    rows1 = lax.broadcasted_iota(jnp.int32, (c, 1), 0)

    def chunk(ci, carry):
        r0 = pl.multiple_of(ci * c, c)
        sl = pl.ds(r0, c)
        q, forget, k = _hgrn_gates(q_ref[sl, :].astype(F32), f_ref[sl, :].astype(F32), lb)
        v = i_ref[sl, :].astype(F32)
        gp = g_ref[sl, :].astype(F32)
        logf = jnp.log(forget)
        gcum = jnp.dot(tril, logf, precision=lax.Precision.HIGHEST, preferred_element_type=F32)

        parts, vs = [], []
        for d in range(w):
            kd = k if d == 0 else pltpu.roll(k, d, 0)
            gd = gcum if d == 0 else pltpu.roll(gcum, d, 0)
            vd = v if d == 0 else pltpu.roll(v, d, 0)
            p = q * kd * jnp.exp(jnp.minimum(gcum - gd, 0.0))
            if d > 0:
                p = jnp.where(rows1 >= d, p, 0.0)
            parts.append(p.astype(BF16))
            vs.append(vd)
        rs = jnp.dot(jnp.concatenate(parts, axis=0), ones, preferred_element_type=F32)
        o = rs[0:c] * vs[0]
        for d in range(1, w):
            o = o + rs[d * c:(d + 1) * c] * vs[d]

        lv = lvl_ref[...]
        scores = jnp.zeros((c, c), F32)
        b = w
        while b < c:
            seg = gcum.reshape(c // (2 * b), 2 * b, HGRN_DK)
            ref = jnp.broadcast_to(seg[:, b - 1:b, :], seg.shape).reshape(c, HGRN_DK)
            ql = (q * jnp.exp(jnp.minimum(gcum - ref, 0.0))).astype(BF16)
            kl = (k * jnp.exp(jnp.minimum(ref - gcum, 0.0))).astype(BF16)
            sc = lax.dot_general(ql, kl, (((1,), (1,)), ((), ())), preferred_element_type=F32)
            scores = jnp.where(lv == b, sc, scores)
            b *= 2
        o = o + jnp.dot(scores.astype(BF16), v.astype(BF16), preferred_element_type=F32)

        st = st_ref[...]
        qs = (q * jnp.exp(gcum)).astype(BF16)
        o = o + lax.dot_general(qs, st.astype(BF16), (((1,), (1,)), ((), ())),
                                preferred_element_type=F32)
        glast = gcum[c - 1:c, :]
        ks = (k * jnp.exp(glast - gcum)).astype(BF16)
        st_ref[...] = st * jnp.exp(glast) + jnp.dot(v.T.astype(BF16), ks, preferred_element_type=F32)

        ms = jnp.mean(o * o, axis=-1, keepdims=True)
        y_ref[sl, :] = (o * lax.rsqrt(ms + LN_EPS) * ng * _silu(gp)).astype(y_ref.dtype)
        return carry

    lax.fori_loop(0, n_chunks, chunk, 0)
    s_ref[0, 0] = st_ref[...].T


def _hgrn_prompt(proj, lb, ng, *, batch, seq, n_rows_out, col0):
    nh = HGRN_HEADS
    blk = lambda off: pl.BlockSpec((seq, HGRN_DK), lambda b, h: (b, col0 + off * nh + h))
    vec = pl.BlockSpec((1, HGRN_DK), lambda b, h: (0, h))
    kern = functools.partial(_hgrn_prompt_kernel, n_chunks=seq // HGRN_CHUNK)
    return pl.pallas_call(
        kern,
        grid=(batch, nh),
        in_specs=[blk(0), blk(1), blk(2), blk(3), vec, vec, pl.BlockSpec(memory_space=pl.ANY)],
        out_specs=[
            pl.BlockSpec((seq, HGRN_DV), lambda b, h: (b, h)),
            pl.BlockSpec((1, 1, HGRN_DK, HGRN_DV), lambda b, h: (b, h, 0, 0)),
        ],
        out_shape=[
            jax.ShapeDtypeStruct((n_rows_out, nh * HGRN_DV), BF16),
            jax.ShapeDtypeStruct((batch, nh, HGRN_DK, HGRN_DV), F32),
        ],
        scratch_shapes=[
            pltpu.VMEM((HGRN_DV, HGRN_DK), F32),
            pltpu.VMEM((HGRN_CHUNK, HGRN_CHUNK), jnp.int32),
        ],
        input_output_aliases={6: 0},
        compiler_params=_params(32, 2),
        name="hgrn_prompt",
    )(proj, proj, proj, proj, lb, ng, jnp.zeros((n_rows_out, nh * HGRN_DV), BF16))


def _hgrn_sample_kernel(q_ref, f_ref, i_ref, g_ref, lb_ref, ng_ref, s_ref, yin_ref,
                        y_ref, so_ref, q_scr, f_scr, k_scr, v_scr, o_scr):
    del yin_ref
    nb = q_ref.shape[0]
    q, forget, k = _hgrn_gates(q_ref[...].astype(F32), f_ref[...].astype(F32), lb_ref[...])
    q_scr[...] = q
    f_scr[...] = forget
    k_scr[...] = k
    v_scr[...] = i_ref[...].astype(F32)
    sq = (HGRN_DK, HGRN_DV)

    def per_seq(n, carry):
        r = pl.ds(n, 1)
        qn, fn, kn, vn = q_scr[r, :], f_scr[r, :], k_scr[r, :], v_scr[r, :]
        for h in range(HGRN_HEADS):
            cs = slice(h * HGRN_DK, (h + 1) * HGRN_DK)
            col = lambda a: jnp.broadcast_to(a[:, cs], sq).T
            s_new = col(fn) * s_ref[n, h] + col(kn) * vn[:, h * HGRN_DV:(h + 1) * HGRN_DV]
            so_ref[n, h] = s_new
            o_scr[n, :, h * HGRN_DV:(h + 1) * HGRN_DV] = jnp.sum(col(qn) * s_new, axis=0, keepdims=True)
        return carry

    lax.fori_loop(0, nb, per_seq, 0)
    o = jnp.concatenate([o_scr[n] for n in range(nb)], axis=0)
    gate = ng_ref[...] * _silu(g_ref[...].astype(F32))
    outs = []
    for h in range(HGRN_HEADS):
        oh = o[:, h * HGRN_DV:(h + 1) * HGRN_DV]
        ms = jnp.mean(oh * oh, axis=-1, keepdims=True)
        outs.append(oh * lax.rsqrt(ms + LN_EPS))
    y_ref[...] = (jnp.concatenate(outs, axis=-1) * gate).astype(y_ref.dtype)


def _hgrn_sample(proj, lb, ng, state, y_b, *, n_prompt_rows, n_seq, col0):
    nb = SAMPLE_SEQ_TILE
    r0 = n_prompt_rows // nb
    wk = HGRN_HEADS * HGRN_DK
    cb0 = col0 * HGRN_DK // wk
    blk = lambda off: pl.BlockSpec((nb, wk), lambda i: (r0 + i, cb0 + off))
    vec = pl.BlockSpec((1, wk), lambda i: (0, 0))
    st_spec = pl.BlockSpec((nb, HGRN_HEADS, HGRN_DK, HGRN_DV), lambda i: (i, 0, 0, 0))
    return pl.pallas_call(
        _hgrn_sample_kernel,
        grid=(n_seq // nb,),
        in_specs=[blk(0), blk(1), blk(2), blk(3), vec, vec, st_spec,
                  pl.BlockSpec(memory_space=pl.ANY)],
        out_specs=[pl.BlockSpec((nb, wk), lambda i: (r0 + i, 0)), st_spec],
        out_shape=[
            jax.ShapeDtypeStruct(y_b.shape, y_b.dtype),
            jax.ShapeDtypeStruct(state.shape, F32),
        ],
        scratch_shapes=[pltpu.VMEM((nb, wk), F32)] * 4 + [pltpu.VMEM((nb, 1, wk), F32)],
        input_output_aliases={7: 0},
        compiler_params=_params(48),
        name="hgrn_sample",
    )(proj, proj, proj, proj, lb, ng, state, y_b)


def _merge_kernel(ya_ref, yb_ref, ga_ref, gb_ref, x_ref, woa_ref, wob_ref, wo_ref,
                  g_ref, b_ref, rw_ref, rb_ref, x1_ref, logit_ref, *, alpha):
    a = jnp.dot(ya_ref[...], woa_ref[...], preferred_element_type=F32)
    b = jnp.dot(yb_ref[...], wob_ref[...], preferred_element_type=F32)
    merged = (jax.nn.sigmoid(ga_ref[...].astype(F32)) * a
              + jax.nn.sigmoid(gb_ref[...].astype(F32)) * b)
    z = alpha * x_ref[...] + jnp.dot(merged.astype(BF16), wo_ref[...], preferred_element_type=F32)
    x1 = _layer_norm(z, g_ref[...], b_ref[...])
    x1_ref[...] = x1
    logit_ref[...] = jnp.dot(x1, rw_ref[...], precision=lax.Precision.HIGHEST,
                             preferred_element_type=F32) + rb_ref[...]


def _merge(y_a, y_b, proj, x, woa, wob, wo, ln_g, ln_b, rw, rb, *, alpha, gate_col_block):
    nt, d = x.shape
    wa = y_a.shape[1]
    tm = MERGE_ROW_TILE
    const = lambda shape: pl.BlockSpec(shape, lambda i: (0,) * len(shape),
                                       pipeline_mode=pl.Buffered(1))
    return pl.pallas_call(
        functools.partial(_merge_kernel, alpha=alpha),
        grid=(nt // tm,),
        in_specs=[
            pl.BlockSpec((tm, wa), lambda i: (i, 0)),
            pl.BlockSpec((tm, wa), lambda i: (i, 0)),
            pl.BlockSpec((tm, d), lambda i: (i, gate_col_block)),
            pl.BlockSpec((tm, d), lambda i: (i, gate_col_block + 1)),
            pl.BlockSpec((tm, d), lambda i: (i, 0)),
            const(woa.shape), const(wob.shape), const(wo.shape),
            const(ln_g.shape), const(ln_b.shape), const(rw.shape), const(rb.shape),
        ],
        out_specs=[
            pl.BlockSpec((tm, d), lambda i: (i, 0)),
            pl.BlockSpec((tm, V7X_LANES), lambda i: (i, 0)),
        ],
        out_shape=[
            jax.ShapeDtypeStruct((nt, d), F32),
            jax.ShapeDtypeStruct((nt, V7X_LANES), F32),
        ],
        compiler_params=_params(48),
        name="merge_ln1",
    )(y_a, y_b, proj, proj, x, woa, wob, wo, ln_g, ln_b, rw, rb)


def _route_kernel(logit_ref, eid_ref, wcol_ref):
    tm = logit_ref.shape[0]
    lt = logit_ref[...].T[0:N_EXPERTS, :]
    e = jnp.exp(lt - jnp.max(lt, axis=0, keepdims=True))
    p = e / jnp.sum(e, axis=0, keepdims=True)
    pr = [p[j:j + 1, :] for j in range(N_EXPERTS)]

    best, gi = None, None
    for g in range(N_EXPERT_GROUPS):
        a, b, c, d = pr[4 * g:4 * g + 4]
        hi1, lo1 = jnp.maximum(a, b), jnp.minimum(a, b)
        hi2, lo2 = jnp.maximum(c, d), jnp.minimum(c, d)
        gs = jnp.maximum(hi1, hi2) + jnp.maximum(jnp.minimum(hi1, hi2), jnp.maximum(lo1, lo2))
        if g == 0:
            best, gi = gs, jnp.zeros((1, tm), jnp.int32)
        else:
            upd = gs > best
            best = jnp.where(upd, gs, best)
            gi = jnp.where(upd, g, gi)

    cand = []
    for j in range(EXPERTS_PER_GROUP):
        cj = pr[j]
        for g in range(1, N_EXPERT_GROUPS):
            cj = jnp.where(gi == g, pr[4 * g + j], cj)
        cand.append(cj)
    w1, i1 = cand[0], jnp.zeros((1, tm), jnp.int32)
    for j in range(1, EXPERTS_PER_GROUP):
        upd = cand[j] > w1
        w1 = jnp.where(upd, cand[j], w1)
        i1 = jnp.where(upd, j, i1)
    w2, i2 = jnp.full((1, tm), -1.0, F32), jnp.zeros((1, tm), jnp.int32)
    for j in range(EXPERTS_PER_GROUP):
        upd = (i1 != j) & (cand[j] > w2)
        w2 = jnp.where(upd, cand[j], w2)
        i2 = jnp.where(upd, j, i2)
    den = w1 + w2
    eid_ref[...] = jnp.concatenate([gi * EXPERTS_PER_GROUP + i1, gi * EXPERTS_PER_GROUP + i2], axis=0)
    wrows = jnp.concatenate([w1 / den, w2 / den, jnp.zeros((V7X_LANES - 2, tm), F32)], axis=0)
    wcol_ref[...] = wrows.T


def _route(logits):
    nt = logits.shape[0]
    tm = ROW_TILE
    return pl.pallas_call(
        _route_kernel,
        grid=(nt // tm,),
        in_specs=[pl.BlockSpec((tm, V7X_LANES), lambda i: (i, 0))],
        out_specs=[
            pl.BlockSpec((2, tm), lambda i: (0, i)),
            pl.BlockSpec((tm, V7X_LANES), lambda i: (i, 0)),
        ],
        out_shape=[
            jax.ShapeDtypeStruct((2, nt), jnp.int32),
            jax.ShapeDtypeStruct((nt, V7X_LANES), F32),
        ],
        compiler_params=_params(32),
        name="route",
    )(logits)


def _dispatch_plan(eid, *, n_rows_padded, tile):
    n_pairs = eid.size
    nt = eid.shape[1]
    e_flat = eid.reshape(n_pairs)
    onehot = (e_flat[:, None] == jnp.arange(N_EXPERTS, dtype=jnp.int32)[None, :]).astype(jnp.int32)
    csum = jnp.cumsum(onehot, axis=0)
    counts = csum[-1]
    rank = jnp.sum(onehot * csum, axis=1) - 1
    padded = ((counts + tile - 1) // tile) * tile
    ends = jnp.cumsum(padded)
    starts = ends - padded
    dest = starts[e_flat] + rank
    tok = jnp.arange(n_pairs, dtype=jnp.int32) % nt
    src = jnp.zeros((n_rows_padded,), jnp.int32).at[dest].set(tok, unique_indices=True)
    n_tiles = n_rows_padded // tile
    tile_start = jnp.arange(n_tiles, dtype=jnp.int32) * tile
    tile_expert = jnp.minimum(
        jnp.sum((tile_start[:, None] >= ends[None, :]).astype(jnp.int32), axis=1), N_EXPERTS - 1)
    n_valid = (ends[-1] // tile).astype(jnp.int32).reshape(1)
    last_expert = tile_expert[jnp.maximum(n_valid[0] - 1, 0)]
    tile_expert = jnp.where(jnp.arange(n_tiles) < n_valid[0], tile_expert, last_expert).astype(jnp.int32)
    return src, dest.reshape(2, nt).astype(jnp.int32), tile_expert, n_valid


def _row_copy(src_hbm, row, dst_vmem, r, sem):
    return pltpu.make_async_copy(src_hbm.at[pl.ds(row, 1), :], dst_vmem.at[pl.ds(r, 1), :], sem)


def _gather_kernel(nvalid_ref, src_ref, x_hbm, o_ref, buf, sem):
    t = pl.program_id(0)
    tm = buf.shape[0]

    @pl.when(t < nvalid_ref[0])
    def _():
        def issue(r, carry):
            _row_copy(x_hbm, src_ref[0, 0, r], buf, r, sem).start()
            return carry
        lax.fori_loop(0, tm, issue, 0, unroll=8)

        def drain(r, carry):
            _row_copy(x_hbm, 0, buf, r, sem).wait()
            return carry
        lax.fori_loop(0, tm, drain, 0, unroll=8)
        o_ref[...] = buf[...].astype(o_ref.dtype)

    @pl.when(t >= nvalid_ref[0])
    def _():
        o_ref[...] = jnp.zeros_like(o_ref)


def _gather_rows(x, src, n_valid, *, tile):
    d = x.shape[1]
    n_tiles = src.shape[0] // tile
    return pl.pallas_call(
        _gather_kernel,
        grid_spec=pltpu.PrefetchScalarGridSpec(
            num_scalar_prefetch=1,
            grid=(n_tiles,),
            in_specs=[
                pl.BlockSpec((1, 1, tile), lambda t, nv: (t, 0, 0), memory_space=pltpu.SMEM),
                pl.BlockSpec(memory_space=pl.ANY),
            ],
            out_specs=pl.BlockSpec((tile, d), lambda t, nv: (t, 0)),
            scratch_shapes=[pltpu.VMEM((tile, d), F32), pltpu.SemaphoreType.DMA(())],
        ),
        out_shape=jax.ShapeDtypeStruct((src.shape[0], d), BF16),
        compiler_params=_params(32),
        name="moe_gather",
    )(n_valid, src.reshape(n_tiles, 1, tile), x)


def _expert_kernel(te_ref, nvalid_ref, x_ref, wg_ref, wu_ref, wd_ref, o_ref):
    del te_ref
    t = pl.program_id(0)

    @pl.when(t < nvalid_ref[0])
    def _():
        x = x_ref[...]
        g = jnp.dot(x, wg_ref[...], preferred_element_type=F32)
        u = jnp.dot(x, wu_ref[...], preferred_element_type=F32)
        h = (_silu(g) * u).astype(BF16)
        o_ref[...] = jnp.dot(h, wd_ref[...], preferred_element_type=F32)

    @pl.when(t >= nvalid_ref[0])
    def _():
        o_ref[...] = jnp.zeros_like(o_ref)


def _experts(xs, w_gate, w_up, w_down, tile_expert, n_valid, *, tile):
    r, d = xs.shape
    dff = w_gate.shape[-1]
    return pl.pallas_call(
        _expert_kernel,
        grid_spec=pltpu.PrefetchScalarGridSpec(
            num_scalar_prefetch=2,
            grid=(r // tile,),
            in_specs=[
                pl.BlockSpec((tile, d), lambda t, te, nv: (t, 0)),
                pl.BlockSpec((None, d, dff), lambda t, te, nv: (te[t], 0, 0)),
                pl.BlockSpec((None, d, dff), lambda t, te, nv: (te[t], 0, 0)),
                pl.BlockSpec((None, dff, d), lambda t, te, nv: (te[t], 0, 0)),
            ],
            out_specs=pl.BlockSpec((tile, d), lambda t, te, nv: (t, 0)),
        ),
        out_shape=jax.ShapeDtypeStruct((r, d), F32),
        compiler_params=_params(48),
        name="moe_experts",
    )(tile_expert, n_valid, xs, w_gate, w_up, w_down)


def _combine_kernel(pos_ref, x_ref, wcol_ref, g_ref, b_ref, ys_hbm, x2_ref, x2b_ref,
                    buf0, buf1, sem, *, alpha):
    tm = x_ref.shape[0]

    def issue(r, carry):
        _row_copy(ys_hbm, pos_ref[0, 0, r], buf0, r, sem).start()
        _row_copy(ys_hbm, pos_ref[0, 1, r], buf1, r, sem).start()
        return carry
    lax.fori_loop(0, tm, issue, 0, unroll=8)

    def drain(r, carry):
        _row_copy(ys_hbm, 0, buf0, r, sem).wait()
        _row_copy(ys_hbm, 0, buf1, r, sem).wait()
        return carry
    lax.fori_loop(0, tm, drain, 0, unroll=8)

    wc = wcol_ref[...]
    z = alpha * x_ref[...] + wc[:, 0:1] * buf0[...] + wc[:, 1:2] * buf1[...]
    x2 = _layer_norm(z, g_ref[...], b_ref[...])
    x2_ref[...] = x2
    x2b_ref[...] = x2.astype(BF16)


def _combine(pos, x1, wcol, ln_g, ln_b, ys, *, alpha):
    nt, d = x1.shape
    tm = COMBINE_ROW_TILE
    n_steps = nt // tm
    pos_t = pos.reshape(2, n_steps, tm).transpose(1, 0, 2)
    return pl.pallas_call(
        functools.partial(_combine_kernel, alpha=alpha),
        grid=(n_steps,),
        in_specs=[
            pl.BlockSpec((1, 2, tm), lambda i: (i, 0, 0), memory_space=pltpu.SMEM),
            pl.BlockSpec((tm, d), lambda i: (i, 0)),
            pl.BlockSpec((tm, V7X_LANES), lambda i: (i, 0)),
            pl.BlockSpec((1, d), lambda i: (0, 0)),
            pl.BlockSpec((1, d), lambda i: (0, 0)),
            pl.BlockSpec(memory_space=pl.ANY),
        ],
        out_specs=[pl.BlockSpec((tm, d), lambda i: (i, 0)), pl.BlockSpec((tm, d), lambda i: (i, 0))],
        out_shape=[jax.ShapeDtypeStruct((nt, d), F32), jax.ShapeDtypeStruct((nt, d), BF16)],
        scratch_shapes=[pltpu.VMEM((tm, d), F32), pltpu.VMEM((tm, d), F32),
                        pltpu.SemaphoreType.DMA(())],
        compiler_params=_params(56),
        name="moe_combine_ln2",
    )(pos_t, x1, wcol, ln_g, ln_b, ys)


def kernel(x_prompt, x_sample, state_hgrn, w_in, sgu_ln_g, sgu_ln_b, w_s, b_s, lb_logits,
           hgrn_norm_g, w_out_a, w_out_b, w_o, ln1_g, ln1_b, router_w, router_b, w_gate, w_up,
           w_down, ln2_g, ln2_b):
    batch, seq, d = x_prompt.shape
    n_seq = x_sample.shape[0]
    depth = w_in.shape[0]
    a_width = sgu_ln_g.shape[1]
    n_prompt = batch * seq
    nt = n_prompt + n_seq
    alpha = (2.0 * depth) ** 0.25
    assert x_sample.shape[1] == 1 and n_seq == SGU_CHUNK and nt % ROW_TILE == 0
    assert a_width == SGU_GROUPS * SGU_CHUNK and d == 2 * a_width

    p = jax.nn.softmax(lb_logits.astype(F32), axis=0)
    lower_bounds = jnp.cumsum(p, axis=0) - p[0:1]

    n_rows_padded = 2 * nt + N_EXPERTS * MOE_ROW_TILE
    n_rows_padded = -(-n_rows_padded // MOE_ROW_TILE) * MOE_ROW_TILE

    w_in_b, woa_b, wob_b, wo_b = (a.astype(BF16) for a in (w_in, w_out_a, w_out_b, w_o))
    wg_b, wu_b, wd_b = (a.astype(BF16) for a in (w_gate, w_up, w_down))
    rw_pad = jnp.zeros((d, V7X_LANES), F32).at[:, :N_EXPERTS].set(router_w.astype(F32))
    rb_pad = jnp.zeros((1, V7X_LANES), F32).at[0, :N_EXPERTS].set(router_b.astype(F32))

    causal = jnp.tril(jnp.ones((SGU_CHUNK, SGU_CHUNK), bool))
    eye = jnp.eye(SGU_CHUNK, dtype=F32)
    gd = a_width // SGU_GROUPS

    x = jnp.concatenate([x_prompt.reshape(n_prompt, d), x_sample.reshape(n_seq, d)], axis=0)
    x_b = x.astype(BF16)
    s_prompt, s_sample, v_sample = [], [], []
    for l in range(depth):
        w_eff = jnp.stack([
            jnp.where(causal[None], w_s[l], 0.0),
            w_s[l][:, 0:1, 0:1] * eye[None],
        ]).astype(BF16)
        bias_eff = jnp.stack([
            jnp.repeat(b_s[l].T, gd, axis=1),
            jnp.broadcast_to(jnp.repeat(b_s[l][:, 0], gd)[None, :], (SGU_CHUNK, a_width)),
        ]).astype(F32)
        lb_l = lower_bounds[l].reshape(1, -1)
        ng_l = hgrn_norm_g[l].astype(F32).reshape(1, -1)

        proj = _matmul(x_b, w_in_b[l], tm=ROW_TILE, tn=PROJ_COL_TILE, out_dtype=BF16)
        y_a, v_rows = _sgu(proj, w_eff, bias_eff, sgu_ln_g[l].reshape(1, -1).astype(F32),
                           sgu_ln_b[l].reshape(1, -1).astype(F32),
                           n_prompt_rows=n_prompt, width=a_width)
        q_col0 = 2 * a_width // HGRN_DK
        y_b, s_p = _hgrn_prompt(proj, lb_l, ng_l, batch=batch, seq=seq, n_rows_out=nt, col0=q_col0)
        y_b, s_s = _hgrn_sample(proj, lb_l, ng_l, state_hgrn[l], y_b,
                                n_prompt_rows=n_prompt, n_seq=n_seq, col0=q_col0)
        gate_col_block = (2 * a_width + 4 * HGRN_HEADS * HGRN_DK) // d
        x1, logits = _merge(y_a, y_b, proj, x, woa_b[l], wob_b[l], wo_b[l],
                            ln1_g[l].reshape(1, -1).astype(F32), ln1_b[l].reshape(1, -1).astype(F32),
                            rw_pad, rb_pad, alpha=alpha, gate_col_block=gate_col_block)
        eid, wcol = _route(logits)
        src, pos, tile_expert, n_valid = _dispatch_plan(
            eid, n_rows_padded=n_rows_padded, tile=MOE_ROW_TILE)
        xs = _gather_rows(x1, src, n_valid, tile=MOE_ROW_TILE)
        ys = _experts(xs, wg_b[l], wu_b[l], wd_b[l], tile_expert, n_valid, tile=MOE_ROW_TILE)
        x, x_b = _combine(pos, x1, wcol, ln2_g[l].reshape(1, -1).astype(F32),
                          ln2_b[l].reshape(1, -1).astype(F32), ys, alpha=alpha)
        s_prompt.append(s_p)
        s_sample.append(s_s)
        v_sample.append(v_rows.reshape(n_seq, 1, a_width))

    y_prompt = x[:n_prompt].reshape(batch, seq, d)
    y_sample = x[n_prompt:].reshape(n_seq, 1, d)
    return (y_prompt, y_sample, jnp.stack(s_prompt), jnp.stack(s_sample), jnp.stack(v_sample))
```

```python
import functools

import jax
import jax.numpy as jnp
from jax import lax
from jax.experimental import pallas as pl
from jax.experimental.pallas import tpu as pltpu

F32 = jnp.float32
BF16 = jnp.bfloat16

SGU_GROUPS = 8
SGU_CHUNK = 128
HGRN_HEADS = 8
HGRN_DK = 128
HGRN_DV = 128
N_EXPERTS = 16
N_EXPERT_GROUPS = 4
EXPERTS_PER_GROUP = N_EXPERTS // N_EXPERT_GROUPS
LN_EPS = 1e-5

V7X_LANES = 128
V7X_SUBLANES = 8
V7X_VMEM_BYTES = 64 * 1024 * 1024

ROW_TILE = 640
MERGE_ROW_TILE = 320
PROJ_COL_TILE = 1024
HGRN_CHUNK = 128
HGRN_BAND = 8
HGRN_PAD = 64
HGRN_HEADS_PER_STEP = 2
SAMPLE_SEQ_TILE = 16
MOE_ROW_TILE = 256
COMBINE_ROW_TILE = 640


def _params(vmem_mb, n_axes=1):
    return pltpu.CompilerParams(
        dimension_semantics=("arbitrary",) * n_axes,
        vmem_limit_bytes=vmem_mb * 1024 * 1024,
    )


def _layer_norm(x, g, b):
    mu = jnp.mean(x, axis=-1, keepdims=True)
    xc = x - mu
    var = jnp.mean(xc * xc, axis=-1, keepdims=True)
    return xc * lax.rsqrt(var + LN_EPS) * g + b


def _silu(x):
    return x * jax.nn.sigmoid(x)


def _matmul_kernel(x_ref, w_ref, o_ref, wb_ref):
    @pl.when(pl.program_id(1) == 0)
    def _():
        wb_ref[...] = w_ref[...].astype(wb_ref.dtype)

    o_ref[...] = jnp.dot(x_ref[...], wb_ref[...], preferred_element_type=F32).astype(o_ref.dtype)


def _matmul(x, w, layer, *, tm, tn, out_dtype):
    m, k = x.shape
    n = w.shape[-1]
    return pl.pallas_call(
        _matmul_kernel,
        grid=(n // tn, m // tm),
        in_specs=[
            pl.BlockSpec((tm, k), lambda j, i: (i, 0)),
            pl.BlockSpec((None, k, tn), lambda j, i: (layer, 0, j)),
        ],
        out_specs=pl.BlockSpec((tm, tn), lambda j, i: (i, j)),
        out_shape=jax.ShapeDtypeStruct((m, n), out_dtype),
        scratch_shapes=[pltpu.VMEM((k, tn), BF16)],
        compiler_params=_params(48, 2),
        name="proj_matmul",
    )(x, w)


def _sgu_kernel(u_ref, v_ref, w_ref, bias_ref, g_ref, b_ref, ya_ref, vrow_ref, *,
                chunks_per_tile, n_prompt_chunks, n_steps):
    i = pl.program_id(0)
    u = jax.nn.gelu(u_ref[...].astype(F32))
    vn = _layer_norm(jax.nn.gelu(v_ref[...].astype(F32)), g_ref[...], b_ref[...])
    gd = vn.shape[-1] // SGU_GROUPS
    for c in range(chunks_per_tile):
        rows = slice(c * SGU_CHUNK, (c + 1) * SGU_CHUNK)
        variant = (i * chunks_per_tile + c >= n_prompt_chunks).astype(jnp.int32)
        w = w_ref[variant]
        vc = vn[rows].astype(BF16)
        parts = [
            jnp.dot(w[g], vc[:, g * gd:(g + 1) * gd], preferred_element_type=F32)
            for g in range(SGU_GROUPS)
        ]
        s = jnp.concatenate(parts, axis=-1) + bias_ref[variant]
        ya_ref[rows, :] = (u[rows] * s).astype(ya_ref.dtype)

    @pl.when(i == n_steps - 1)
    def _():
        last = slice((chunks_per_tile - 1) * SGU_CHUNK, chunks_per_tile * SGU_CHUNK)
        vrow_ref[...] = vn[last]


def _sgu(proj, w_eff, bias_eff, ln_g, ln_b, *, n_prompt_rows, width):
    nt = proj.shape[0]
    tr = ROW_TILE
    n_steps = nt // tr
    kern = functools.partial(
        _sgu_kernel, chunks_per_tile=tr // SGU_CHUNK,
        n_prompt_chunks=n_prompt_rows // SGU_CHUNK, n_steps=n_steps)
    return pl.pallas_call(
        kern,
        grid=(n_steps,),
        in_specs=[
            pl.BlockSpec((tr, width), lambda i: (i, 0)),
            pl.BlockSpec((tr, width), lambda i: (i, 1)),
            pl.BlockSpec(w_eff.shape, lambda i: (0, 0, 0, 0)),
            pl.BlockSpec(bias_eff.shape, lambda i: (0, 0, 0)),
            pl.BlockSpec((1, width), lambda i: (0, 0)),
            pl.BlockSpec((1, width), lambda i: (0, 0)),
        ],
        out_specs=[
            pl.BlockSpec((tr, width), lambda i: (i, 0)),
            pl.BlockSpec((SGU_CHUNK, width), lambda i: (0, 0)),
        ],
        out_shape=[
            jax.ShapeDtypeStruct((nt, width), BF16),
            jax.ShapeDtypeStruct((SGU_CHUNK, width), F32),
        ],
        compiler_params=_params(40),
        name="sgu",
    )(proj, proj, w_eff, bias_eff, ln_g, ln_b)


def _hgrn_gates(qp, z, lb):
    q = _silu(qp)
    e = jnp.exp(-jnp.abs(z))
    s_big = 1.0 / (1.0 + e)
    s_small = e * s_big
    pos = z >= 0.0
    one_m_lb = 1.0 - lb
    forget = lb + one_m_lb * jnp.where(pos, s_big, s_small)
    k = one_m_lb * jnp.where(pos, s_small, s_big)
    return q, forget, k


def _hgrn_prompt_kernel(q_ref, f_ref, i_ref, g_ref, lb_ref, ng_ref, yin_ref, y_ref, s_ref,
                        st_ref, lvl_ref, kpad, gpad, vpad, *, n_chunks):
    del yin_ref
    c = HGRN_CHUNK
    w = HGRN_BAND
    pad = HGRN_PAD
    nh = HGRN_HEADS_PER_STEP
    dk = HGRN_DK

    row = lax.broadcasted_iota(jnp.int32, (c, c), 0)
    col = lax.broadcasted_iota(jnp.int32, (c, c), 1)
    x = row ^ col
    lvl = jnp.where(x >= 64, 64, jnp.where(x >= 32, 32, jnp.where(x >= 16, 16, jnp.where(x >= 8, 8, 0))))
    lvl_ref[...] = jnp.where(row - col >= w, lvl, 0)
    st_ref[...] = jnp.zeros_like(st_ref)
    for ref in (kpad, gpad, vpad):
        ref[:, 0:pad, :] = jnp.zeros((nh, pad, dk), F32)
    r2 = lax.broadcasted_iota(jnp.int32, (2 * dk, 2 * dk), 0)
    c2 = lax.broadcasted_iota(jnp.int32, (2 * dk, 2 * dk), 1)
    pair_ones = jnp.where((r2 >= dk) == (c2 >= dk), 1.0, 0.0).astype(BF16)
    cur = pl.ds(pad, c)

    def head_chunk(hh, sl):
        cs = slice(hh * dk, (hh + 1) * dk)
        q, forget, k = _hgrn_gates(q_ref[sl, cs].astype(F32), f_ref[sl, cs].astype(F32),
                                   lb_ref[:, cs])
        v = i_ref[sl, cs].astype(F32)
        gcum = jnp.log(forget)
        shift = 1
        while shift < c:
            gpad[hh, cur, :] = gcum
            gcum = gcum + gpad[hh, pl.ds(pad - shift, c), :]
            shift *= 2
        gpad[hh, cur, :] = gcum
        kpad[hh, cur, :] = k
        vpad[hh, cur, :] = v

        parts = []
        for d in range(w):
            if d == 0:
                p = q * k
            else:
                back = pl.ds(pad - d, c)
                p = q * kpad[hh, back, :] * jnp.exp(gcum - gpad[hh, back, :])
            parts.append(p.astype(BF16))
        pairs = [jnp.concatenate(parts[2 * j:2 * j + 2], axis=1) for j in range(w // 2)]
        rs = jnp.dot(jnp.concatenate(pairs, axis=0), pair_ones, preferred_element_type=F32)
        o = rs[0:c, 0:dk] * v
        for d in range(1, w):
            j, half = divmod(d, 2)
            o = o + rs[j * c:(j + 1) * c, half * dk:(half + 1) * dk] * vpad[hh, pl.ds(pad - d, c), :]

        lv = lvl_ref[...]
        scores = jnp.zeros((c, c), F32)
        b = w
        while b < c:
            seg = gcum.reshape(c // (2 * b), 2 * b, dk)
            ref = jnp.broadcast_to(seg[:, b - 1:b, :], seg.shape).reshape(c, dk)
            a = jnp.exp(-jnp.abs(gcum - ref))
            sc = lax.dot_general((q * a).astype(BF16), (k * a).astype(BF16),
                                 (((1,), (1,)), ((), ())), preferred_element_type=F32)
            scores = jnp.where(lv == b, sc, scores)
            b *= 2
        o = o + jnp.dot(scores.astype(BF16), v.astype(BF16), preferred_element_type=F32)

        st = st_ref[hh]
        qs = (q * jnp.exp(gcum)).astype(BF16)
        o = o + lax.dot_general(qs, st.astype(BF16), (((1,), (1,)), ((), ())),
                                preferred_element_type=F32)
        glast = gcum[c - 1:c, :]
        ks = (k * jnp.exp(glast - gcum)).astype(BF16)
        st_ref[hh] = st * jnp.exp(glast) + jnp.dot(v.T.astype(BF16), ks, preferred_element_type=F32)

        ms = jnp.mean(o * o, axis=-1, keepdims=True)
        gate = ng_ref[:, cs] * _silu(g_ref[sl, cs].astype(F32))
        y_ref[sl, cs] = (o * lax.rsqrt(ms + LN_EPS) * gate).astype(y_ref.dtype)

    def chunk(ci, carry):
        sl = pl.ds(pl.multiple_of(ci * c, c), c)
        for hh in range(nh):
            head_chunk(hh, sl)
        return carry

    lax.fori_loop(0, n_chunks, chunk, 0)
    for hh in range(nh):
        s_ref[0, hh] = st_ref[hh].T


def _hgrn_prompt(proj, lb, ng, *, batch, seq, n_rows_out, col0):
    nh = HGRN_HEADS
    hs = HGRN_HEADS_PER_STEP
    wb = hs * HGRN_DK
    cb0 = col0 * HGRN_DK // wb
    nhb = nh // hs
    blk = lambda off: pl.BlockSpec((seq, wb), lambda b, h: (b, cb0 + off * nhb + h))
    vec = pl.BlockSpec((1, wb), lambda b, h: (0, h))
    kern = functools.partial(_hgrn_prompt_kernel, n_chunks=seq // HGRN_CHUNK)
    padded = pltpu.VMEM((hs, HGRN_PAD + HGRN_CHUNK, HGRN_DK), F32)
    return pl.pallas_call(
        kern,
        grid=(batch, nhb),
        in_specs=[blk(0), blk(1), blk(2), blk(3), vec, vec, pl.BlockSpec(memory_space=pl.ANY)],
        out_specs=[
            pl.BlockSpec((seq, wb), lambda b, h: (b, h)),
            pl.BlockSpec((1, hs, HGRN_DK, HGRN_DV), lambda b, h: (b, h, 0, 0)),
        ],
        out_shape=[
            jax.ShapeDtypeStruct((n_rows_out, nh * HGRN_DV), BF16),
            jax.ShapeDtypeStruct((batch, nh, HGRN_DK, HGRN_DV), F32),
        ],
        scratch_shapes=[
            pltpu.VMEM((hs, HGRN_DV, HGRN_DK), F32),
            pltpu.VMEM((HGRN_CHUNK, HGRN_CHUNK), jnp.int32),
            padded, padded, padded,
        ],
        input_output_aliases={6: 0},
        compiler_params=_params(32, 2),
        name="hgrn_prompt",
    )(proj, proj, proj, proj, lb, ng, jnp.zeros((n_rows_out, nh * HGRN_DV), BF16))


def _hgrn_sample_kernel(q_ref, f_ref, i_ref, g_ref, lb_ref, ng_ref, s_ref, yin_ref, sall_ref,
                        y_ref, so_ref, q_scr, f_scr, k_scr, v_scr, o_scr):
    del yin_ref, sall_ref
    nb = q_ref.shape[0]
    q, forget, k = _hgrn_gates(q_ref[...].astype(F32), f_ref[...].astype(F32), lb_ref[...])
    q_scr[...] = q
    f_scr[...] = forget
    k_scr[...] = k
    v_scr[...] = i_ref[...].astype(F32)
    sq = (HGRN_DK, HGRN_DV)

    def per_seq(n, carry):
        r = pl.ds(n, 1)
        qn, fn, kn, vn = q_scr[r, :], f_scr[r, :], k_scr[r, :], v_scr[r, :]
        for h in range(HGRN_HEADS):
            cs = slice(h * HGRN_DK, (h + 1) * HGRN_DK)
            col = lambda a: jnp.broadcast_to(a[:, cs], sq).T
            s_new = col(fn) * s_ref[n, h] + col(kn) * vn[:, h * HGRN_DV:(h + 1) * HGRN_DV]
            so_ref[n, h] = s_new
            o_scr[n, :, h * HGRN_DV:(h + 1) * HGRN_DV] = jnp.sum(col(qn) * s_new, axis=0, keepdims=True)
        return carry

    lax.fori_loop(0, nb, per_seq, 0)
    o = jnp.concatenate([o_scr[n] for n in range(nb)], axis=0)
    gate = ng_ref[...] * _silu(g_ref[...].astype(F32))
    outs = []
    for h in range(HGRN_HEADS):
        oh = o[:, h * HGRN_DV:(h + 1) * HGRN_DV]
        ms = jnp.mean(oh * oh, axis=-1, keepdims=True)
        outs.append(oh * lax.rsqrt(ms + LN_EPS))
    y_ref[...] = (jnp.concatenate(outs, axis=-1) * gate).astype(y_ref.dtype)


def _hgrn_sample(proj, lb, ng, state, y_b, state_out, layer, *, n_prompt_rows, n_seq, col0):
    nb = SAMPLE_SEQ_TILE
    r0 = n_prompt_rows // nb
    wk = HGRN_HEADS * HGRN_DK
    cb0 = col0 * HGRN_DK // wk
    blk = lambda off: pl.BlockSpec((nb, wk), lambda i: (r0 + i, cb0 + off))
    vec = pl.BlockSpec((1, wk), lambda i: (0, 0))
    st_spec = pl.BlockSpec((None, nb, HGRN_HEADS, HGRN_DK, HGRN_DV), lambda i: (layer, i, 0, 0, 0))
    anywhere = pl.BlockSpec(memory_space=pl.ANY)
    return pl.pallas_call(
        _hgrn_sample_kernel,
        grid=(n_seq // nb,),
        in_specs=[blk(0), blk(1), blk(2), blk(3), vec, vec, st_spec, anywhere, anywhere],
        out_specs=[pl.BlockSpec((nb, wk), lambda i: (r0 + i, 0)), st_spec],
        out_shape=[
            jax.ShapeDtypeStruct(y_b.shape, y_b.dtype),
            jax.ShapeDtypeStruct(state_out.shape, state_out.dtype),
        ],
        scratch_shapes=[pltpu.VMEM((nb, wk), F32)] * 4 + [pltpu.VMEM((nb, 1, wk), F32)],
        input_output_aliases={7: 0, 8: 1},
        compiler_params=_params(48),
        name="hgrn_sample",
    )(proj, proj, proj, proj, lb, ng, state, y_b, state_out)


def _merge_kernel(ya_ref, yb_ref, ga_ref, gb_ref, x_ref, woa_ref, wob_ref, wo_ref,
                  g_ref, b_ref, rw_ref, rb_ref, x1_ref, logit_ref, *, alpha):
    a = jnp.dot(ya_ref[...], woa_ref[...], preferred_element_type=F32)
    b = jnp.dot(yb_ref[...], wob_ref[...], preferred_element_type=F32)
    merged = (jax.nn.sigmoid(ga_ref[...].astype(F32)) * a
              + jax.nn.sigmoid(gb_ref[...].astype(F32)) * b)
    z = alpha * x_ref[...] + jnp.dot(merged.astype(BF16), wo_ref[...], preferred_element_type=F32)
    x1 = _layer_norm(z, g_ref[...], b_ref[...])
    x1_ref[...] = x1
    hi = x1.astype(BF16)
    mid = (x1 - hi.astype(F32)).astype(BF16)
    logit_ref[...] = jnp.dot(jnp.concatenate([hi, mid, hi], axis=1), rw_ref[...],
                             preferred_element_type=F32) + rb_ref[...]


def _merge(y_a, y_b, proj, x, woa, wob, wo, ln_g, ln_b, rw, rb, layer, *, alpha, gate_col_block):
    nt, d = x.shape
    wa = y_a.shape[1]
    tm = MERGE_ROW_TILE
    const = lambda shape: pl.BlockSpec(shape, lambda i: (0,) * len(shape),
                                       pipeline_mode=pl.Buffered(1))
    stacked = lambda a: pl.BlockSpec((None,) + a.shape[1:], lambda i: (layer, 0, 0),
                                     pipeline_mode=pl.Buffered(1))
    return pl.pallas_call(
        functools.partial(_merge_kernel, alpha=alpha),
        grid=(nt // tm,),
        in_specs=[
            pl.BlockSpec((tm, wa), lambda i: (i, 0)),
            pl.BlockSpec((tm, wa), lambda i: (i, 0)),
            pl.BlockSpec((tm, d), lambda i: (i, gate_col_block)),
            pl.BlockSpec((tm, d), lambda i: (i, gate_col_block + 1)),
            pl.BlockSpec((tm, d), lambda i: (i, 0)),
            stacked(woa), stacked(wob), stacked(wo),
            const(ln_g.shape), const(ln_b.shape), const(rw.shape), const(rb.shape),
        ],
        out_specs=[
            pl.BlockSpec((tm, d), lambda i: (i, 0)),
            pl.BlockSpec((tm, V7X_LANES), lambda i: (i, 0)),
        ],
        out_shape=[
            jax.ShapeDtypeStruct((nt, d), F32),
            jax.ShapeDtypeStruct((nt, V7X_LANES), F32),
        ],
        compiler_params=_params(48),
        name="merge_ln1",
    )(y_a, y_b, proj, proj, x, woa, wob, wo, ln_g, ln_b, rw, rb)


def _route_kernel(logit_ref, eid_ref, wcol_ref):
    tm = logit_ref.shape[0]
    lt = logit_ref[...].T[0:N_EXPERTS, :]
    e = jnp.exp(lt - jnp.max(lt, axis=0, keepdims=True))
    p = e / jnp.sum(e, axis=0, keepdims=True)
    pr = [p[j:j + 1, :] for j in range(N_EXPERTS)]

    best, gi = None, None
    for g in range(N_EXPERT_GROUPS):
        a, b, c, d = pr[4 * g:4 * g + 4]
        hi1, lo1 = jnp.maximum(a, b), jnp.minimum(a, b)
        hi2, lo2 = jnp.maximum(c, d), jnp.minimum(c, d)
        gs = jnp.maximum(hi1, hi2) + jnp.maximum(jnp.minimum(hi1, hi2), jnp.maximum(lo1, lo2))
        if g == 0:
            best, gi = gs, jnp.zeros((1, tm), jnp.int32)
        else:
            upd = gs > best
            best = jnp.where(upd, gs, best)
            gi = jnp.where(upd, g, gi)

    cand = []
    for j in range(EXPERTS_PER_GROUP):
        cj = pr[j]
        for g in range(1, N_EXPERT_GROUPS):
            cj = jnp.where(gi == g, pr[4 * g + j], cj)
        cand.append(cj)
    w1, i1 = cand[0], jnp.zeros((1, tm), jnp.int32)
    for j in range(1, EXPERTS_PER_GROUP):
        upd = cand[j] > w1
        w1 = jnp.where(upd, cand[j], w1)
        i1 = jnp.where(upd, j, i1)
    w2, i2 = jnp.full((1, tm), -1.0, F32), jnp.zeros((1, tm), jnp.int32)
    for j in range(EXPERTS_PER_GROUP):
        upd = (i1 != j) & (cand[j] > w2)
        w2 = jnp.where(upd, cand[j], w2)
        i2 = jnp.where(upd, j, i2)
    den = w1 + w2
    eid_ref[...] = jnp.concatenate([gi * EXPERTS_PER_GROUP + i1, gi * EXPERTS_PER_GROUP + i2], axis=0)
    wrows = jnp.concatenate([w1 / den, w2 / den, jnp.zeros((V7X_LANES - 2, tm), F32)], axis=0)
    wcol_ref[...] = wrows.T


def _route(logits):
    nt = logits.shape[0]
    tm = ROW_TILE
    return pl.pallas_call(
        _route_kernel,
        grid=(nt // tm,),
        in_specs=[pl.BlockSpec((tm, V7X_LANES), lambda i: (i, 0))],
        out_specs=[
            pl.BlockSpec((2, tm), lambda i: (0, i)),
            pl.BlockSpec((tm, V7X_LANES), lambda i: (i, 0)),
        ],
        out_shape=[
            jax.ShapeDtypeStruct((2, nt), jnp.int32),
            jax.ShapeDtypeStruct((nt, V7X_LANES), F32),
        ],
        compiler_params=_params(32),
        name="route",
    )(logits)


def _dispatch_plan(eid, *, n_rows_padded, tile):
    n_pairs = eid.size
    nt = eid.shape[1]
    e_flat = eid.reshape(n_pairs)
    onehot = (e_flat[:, None] == jnp.arange(N_EXPERTS, dtype=jnp.int32)[None, :]).astype(jnp.int32)
    csum = jnp.cumsum(onehot, axis=0)
    counts = csum[-1]
    rank = jnp.sum(onehot * csum, axis=1) - 1
    padded = ((counts + tile - 1) // tile) * tile
    ends = jnp.cumsum(padded)
    starts = ends - padded
    dest = starts[e_flat] + rank
    tok = jnp.arange(n_pairs, dtype=jnp.int32) % nt
    src = jnp.zeros((n_rows_padded,), jnp.int32).at[dest].set(tok, unique_indices=True)
    n_tiles = n_rows_padded // tile
    tile_start = jnp.arange(n_tiles, dtype=jnp.int32) * tile
    tile_expert = jnp.minimum(
        jnp.sum((tile_start[:, None] >= ends[None, :]).astype(jnp.int32), axis=1), N_EXPERTS - 1)
    n_valid = (ends[-1] // tile).astype(jnp.int32).reshape(1)
    last_expert = tile_expert[jnp.maximum(n_valid[0] - 1, 0)]
    tile_expert = jnp.where(jnp.arange(n_tiles) < n_valid[0], tile_expert, last_expert).astype(jnp.int32)
    return src, dest.reshape(2, nt).astype(jnp.int32), tile_expert, n_valid


def _row_copy(src_hbm, row, dst_vmem, r, sem):
    return pltpu.make_async_copy(src_hbm.at[pl.ds(row, 1), :], dst_vmem.at[pl.ds(r, 1), :], sem)


def _gather_kernel(nvalid_ref, src_ref, x_hbm, o_ref, buf, sem):
    t = pl.program_id(0)
    tm = buf.shape[0]

    @pl.when(t < nvalid_ref[0])
    def _():
        def issue(r, carry):
            _row_copy(x_hbm, src_ref[0, 0, r], buf, r, sem).start()
            return carry
        lax.fori_loop(0, tm, issue, 0, unroll=8)

        def drain(r, carry):
            _row_copy(x_hbm, 0, buf, r, sem).wait()
            return carry
        lax.fori_loop(0, tm, drain, 0, unroll=8)
        o_ref[...] = buf[...].astype(o_ref.dtype)

    @pl.when(t >= nvalid_ref[0])
    def _():
        o_ref[...] = jnp.zeros_like(o_ref)


def _gather_rows(x, src, n_valid, *, tile):
    d = x.shape[1]
    n_tiles = src.shape[0] // tile
    return pl.pallas_call(
        _gather_kernel,
        grid_spec=pltpu.PrefetchScalarGridSpec(
            num_scalar_prefetch=1,
            grid=(n_tiles,),
            in_specs=[
                pl.BlockSpec((1, 1, tile), lambda t, nv: (t, 0, 0), memory_space=pltpu.SMEM),
                pl.BlockSpec(memory_space=pl.ANY),
            ],
            out_specs=pl.BlockSpec((tile, d), lambda t, nv: (t, 0)),
            scratch_shapes=[pltpu.VMEM((tile, d), F32), pltpu.SemaphoreType.DMA(())],
        ),
        out_shape=jax.ShapeDtypeStruct((src.shape[0], d), BF16),
        compiler_params=_params(32),
        name="moe_gather",
    )(n_valid, src.reshape(n_tiles, 1, tile), x)


def _expert_kernel(te_ref, nvalid_ref, x_ref, wg_ref, wu_ref, wd_ref, o_ref):
    del te_ref
    t = pl.program_id(0)

    @pl.when(t < nvalid_ref[0])
    def _():
        x = x_ref[...]
        g = jnp.dot(x, wg_ref[...], preferred_element_type=F32)
        u = jnp.dot(x, wu_ref[...], preferred_element_type=F32)
        h = (_silu(g) * u).astype(BF16)
        o_ref[...] = jnp.dot(h, wd_ref[...], preferred_element_type=F32)

    @pl.when(t >= nvalid_ref[0])
    def _():
        o_ref[...] = jnp.zeros_like(o_ref)


def _experts(xs, w_gate, w_up, w_down, tile_expert, n_valid, layer, *, tile):
    r, d = xs.shape
    dff = w_gate.shape[-1]
    return pl.pallas_call(
        _expert_kernel,
        grid_spec=pltpu.PrefetchScalarGridSpec(
            num_scalar_prefetch=2,
            grid=(r // tile,),
            in_specs=[
                pl.BlockSpec((tile, d), lambda t, te, nv: (t, 0)),
                pl.BlockSpec((None, None, d, dff), lambda t, te, nv: (layer, te[t], 0, 0)),
                pl.BlockSpec((None, None, d, dff), lambda t, te, nv: (layer, te[t], 0, 0)),
                pl.BlockSpec((None, None, dff, d), lambda t, te, nv: (layer, te[t], 0, 0)),
            ],
            out_specs=pl.BlockSpec((tile, d), lambda t, te, nv: (t, 0)),
        ),
        out_shape=jax.ShapeDtypeStruct((r, d), F32),
        compiler_params=_params(48),
        name="moe_experts",
    )(tile_expert, n_valid, xs, w_gate, w_up, w_down)


def _combine_kernel(pos_ref, x_ref, wcol_ref, g_ref, b_ref, ys_hbm, x2_ref, x2b_ref,
                    buf0, buf1, sem, *, alpha):
    tm = x_ref.shape[0]

    def issue(r, carry):
        _row_copy(ys_hbm, pos_ref[0, 0, r], buf0, r, sem).start()
        _row_copy(ys_hbm, pos_ref[0, 1, r], buf1, r, sem).start()
        return carry
    lax.fori_loop(0, tm, issue, 0, unroll=8)

    def drain(r, carry):
        _row_copy(ys_hbm, 0, buf0, r, sem).wait()
        _row_copy(ys_hbm, 0, buf1, r, sem).wait()
        return carry
    lax.fori_loop(0, tm, drain, 0, unroll=8)

    wc = wcol_ref[...]
    z = alpha * x_ref[...] + wc[:, 0:1] * buf0[...] + wc[:, 1:2] * buf1[...]
    x2 = _layer_norm(z, g_ref[...], b_ref[...])
    x2_ref[...] = x2
    x2b_ref[...] = x2.astype(BF16)


def _combine(pos, x1, wcol, ln_g, ln_b, ys, *, alpha):
    nt, d = x1.shape
    tm = COMBINE_ROW_TILE
    n_steps = nt // tm
    pos_t = pos.reshape(2, n_steps, tm).transpose(1, 0, 2)
    return pl.pallas_call(
        functools.partial(_combine_kernel, alpha=alpha),
        grid=(n_steps,),
        in_specs=[
            pl.BlockSpec((1, 2, tm), lambda i: (i, 0, 0), memory_space=pltpu.SMEM),
            pl.BlockSpec((tm, d), lambda i: (i, 0)),
            pl.BlockSpec((tm, V7X_LANES), lambda i: (i, 0)),
            pl.BlockSpec((1, d), lambda i: (0, 0)),
            pl.BlockSpec((1, d), lambda i: (0, 0)),
            pl.BlockSpec(memory_space=pl.ANY),
        ],
        out_specs=[pl.BlockSpec((tm, d), lambda i: (i, 0)), pl.BlockSpec((tm, d), lambda i: (i, 0))],
        out_shape=[jax.ShapeDtypeStruct((nt, d), F32), jax.ShapeDtypeStruct((nt, d), BF16)],
        scratch_shapes=[pltpu.VMEM((tm, d), F32), pltpu.VMEM((tm, d), F32),
                        pltpu.SemaphoreType.DMA(())],
        compiler_params=_params(56),
        name="moe_combine_ln2",
    )(pos_t, x1, wcol, ln_g, ln_b, ys)


def kernel(x_prompt, x_sample, state_hgrn, w_in, sgu_ln_g, sgu_ln_b, w_s, b_s, lb_logits,
           hgrn_norm_g, w_out_a, w_out_b, w_o, ln1_g, ln1_b, router_w, router_b, w_gate, w_up,
           w_down, ln2_g, ln2_b):
    batch, seq, d = x_prompt.shape
    n_seq = x_sample.shape[0]
    depth = w_in.shape[0]
    a_width = sgu_ln_g.shape[1]
    n_prompt = batch * seq
    nt = n_prompt + n_seq
    alpha = (2.0 * depth) ** 0.25
    assert x_sample.shape[1] == 1 and n_seq == SGU_CHUNK and nt % ROW_TILE == 0
    assert a_width == SGU_GROUPS * SGU_CHUNK and d == 2 * a_width

    p = jax.nn.softmax(lb_logits.astype(F32), axis=0)
    lower_bounds = jnp.cumsum(p, axis=0) - p[0:1]

    n_rows_padded = 2 * nt + N_EXPERTS * MOE_ROW_TILE
    n_rows_padded = -(-n_rows_padded // MOE_ROW_TILE) * MOE_ROW_TILE

    woa_b, wob_b, wo_b = (a.astype(BF16) for a in (w_out_a, w_out_b, w_o))
    wg_b, wu_b, wd_b = (a.astype(BF16) for a in (w_gate, w_up, w_down))
    rw_pad = jnp.zeros((d, V7X_LANES), F32).at[:, :N_EXPERTS].set(router_w.astype(F32))
    rw_hi = rw_pad.astype(BF16)
    rw_mid = (rw_pad - rw_hi.astype(F32)).astype(BF16)
    rw3 = jnp.concatenate([rw_hi, rw_hi, rw_mid], axis=0)
    rb_pad = jnp.zeros((1, V7X_LANES), F32).at[0, :N_EXPERTS].set(router_b.astype(F32))
    s_sample = jnp.zeros(state_hgrn.shape, F32)

    causal = jnp.tril(jnp.ones((SGU_CHUNK, SGU_CHUNK), bool))
    eye = jnp.eye(SGU_CHUNK, dtype=F32)
    gd = a_width // SGU_GROUPS

    x = jnp.concatenate([x_prompt.reshape(n_prompt, d), x_sample.reshape(n_seq, d)], axis=0)
    x_b = x.astype(BF16)
    s_prompt, v_sample = [], []
    for l in range(depth):
        w_eff = jnp.stack([
            jnp.where(causal[None], w_s[l], 0.0),
            w_s[l][:, 0:1, 0:1] * eye[None],
        ]).astype(BF16)
        bias_eff = jnp.stack([
            jnp.repeat(b_s[l].T, gd, axis=1),
            jnp.broadcast_to(jnp.repeat(b_s[l][:, 0], gd)[None, :], (SGU_CHUNK, a_width)),
        ]).astype(F32)
        lb_l = lower_bounds[l].reshape(1, -1)
        ng_l = hgrn_norm_g[l].astype(F32).reshape(1, -1)

        proj = _matmul(x_b, w_in, l, tm=ROW_TILE, tn=PROJ_COL_TILE, out_dtype=BF16)
        y_a, v_rows = _sgu(proj, w_eff, bias_eff, sgu_ln_g[l].reshape(1, -1).astype(F32),
                           sgu_ln_b[l].reshape(1, -1).astype(F32),
                           n_prompt_rows=n_prompt, width=a_width)
        q_col0 = 2 * a_width // HGRN_DK
        y_b, s_p = _hgrn_prompt(proj, lb_l, ng_l, batch=batch, seq=seq, n_rows_out=nt, col0=q_col0)
        y_b, s_sample = _hgrn_sample(proj, lb_l, ng_l, state_hgrn, y_b, s_sample, l,
                                     n_prompt_rows=n_prompt, n_seq=n_seq, col0=q_col0)
        gate_col_block = (2 * a_width + 4 * HGRN_HEADS * HGRN_DK) // d
        x1, logits = _merge(y_a, y_b, proj, x, woa_b, wob_b, wo_b,
                            ln1_g[l].reshape(1, -1).astype(F32), ln1_b[l].reshape(1, -1).astype(F32),
                            rw3, rb_pad, l, alpha=alpha, gate_col_block=gate_col_block)
        eid, wcol = _route(logits)
        src, pos, tile_expert, n_valid = _dispatch_plan(
            eid, n_rows_padded=n_rows_padded, tile=MOE_ROW_TILE)
        xs = _gather_rows(x1, src, n_valid, tile=MOE_ROW_TILE)
        ys = _experts(xs, wg_b, wu_b, wd_b, tile_expert, n_valid, l, tile=MOE_ROW_TILE)
        x, x_b = _combine(pos, x1, wcol, ln2_g[l].reshape(1, -1).astype(F32),
                          ln2_b[l].reshape(1, -1).astype(F32), ys, alpha=alpha)
        s_prompt.append(s_p)
        v_sample.append(v_rows.reshape(n_seq, 1, a_width))

    y_prompt = x[:n_prompt].reshape(batch, seq, d)
    y_sample = x[n_prompt:].reshape(n_seq, 1, d)
    return (y_prompt, y_sample, jnp.stack(s_prompt), s_sample, jnp.stack(v_sample))
```

```python
import functools

import jax
import jax.numpy as jnp
from jax import lax
from jax.experimental import pallas as pl
from jax.experimental.pallas import tpu as pltpu

F32 = jnp.float32
BF16 = jnp.bfloat16

SGU_GROUPS = 8
SGU_CHUNK = 128
HGRN_HEADS = 8
HGRN_DK = 128
HGRN_DV = 128
N_EXPERTS = 16
N_EXPERT_GROUPS = 4
EXPERTS_PER_GROUP = N_EXPERTS // N_EXPERT_GROUPS
LN_EPS = 1e-5

V7X_LANES = 128
V7X_SUBLANES = 8
V7X_VMEM_BYTES = 64 * 1024 * 1024

ROW_TILE = 640
PROJ_ROW_TILE = 832
MERGE_ROW_TILE = 416
COMBINE_PARTS = 2
DMA_UNROLL = 8
PROJ_COL_TILE = 1024
HGRN_CHUNK = 128
HGRN_BAND = 8
HGRN_PAD = 64
HGRN_HEADS_PER_STEP = 2
SAMPLE_SEQ_TILE = 16
MOE_ROW_TILE = 256
COMBINE_ROW_TILE = 640


def _params(vmem_mb, n_axes=1):
    return pltpu.CompilerParams(
        dimension_semantics=("arbitrary",) * n_axes,
        vmem_limit_bytes=vmem_mb * 1024 * 1024,
    )


def _layer_norm(x, g, b):
    mu = jnp.mean(x, axis=-1, keepdims=True)
    xc = x - mu
    var = jnp.mean(xc * xc, axis=-1, keepdims=True)
    return xc * lax.rsqrt(var + LN_EPS) * g + b


def _silu(x):
    return x * jax.nn.sigmoid(x)


def _matmul_kernel(x_ref, w_ref, o_ref, wb_ref):
    @pl.when(pl.program_id(1) == 0)
    def _():
        wb_ref[...] = w_ref[...].astype(wb_ref.dtype)

    o_ref[...] = jnp.dot(x_ref[...], wb_ref[...], preferred_element_type=F32).astype(o_ref.dtype)


def _matmul(x, w, layer, *, tm, tn, out_dtype):
    m, k = x.shape
    n = w.shape[-1]
    return pl.pallas_call(
        _matmul_kernel,
        grid=(n // tn, m // tm),
        in_specs=[
            pl.BlockSpec((tm, k), lambda j, i: (i, 0)),
            pl.BlockSpec((None, k, tn), lambda j, i: (layer, 0, j)),
        ],
        out_specs=pl.BlockSpec((tm, tn), lambda j, i: (i, j)),
        out_shape=jax.ShapeDtypeStruct((m, n), out_dtype),
        scratch_shapes=[pltpu.VMEM((k, tn), BF16)],
        compiler_params=_params(48, 2),
        name="proj_matmul",
    )(x, w)


def _sgu_kernel(u_ref, v_ref, w_ref, bias_ref, g_ref, b_ref, ya_ref, vrow_ref, *,
                chunks_per_tile, n_prompt_chunks, n_steps):
    i = pl.program_id(0)
    u = jax.nn.gelu(u_ref[...].astype(F32))
    vn = _layer_norm(jax.nn.gelu(v_ref[...].astype(F32)), g_ref[...], b_ref[...])
    gd = vn.shape[-1] // SGU_GROUPS
    for c in range(chunks_per_tile):
        rows = slice(c * SGU_CHUNK, (c + 1) * SGU_CHUNK)
        variant = (i * chunks_per_tile + c >= n_prompt_chunks).astype(jnp.int32)
        w = w_ref[variant]
        vc = vn[rows].astype(BF16)
        parts = [
            jnp.dot(w[g], vc[:, g * gd:(g + 1) * gd], preferred_element_type=F32)
            for g in range(SGU_GROUPS)
        ]
        s = jnp.concatenate(parts, axis=-1) + bias_ref[variant]
        ya_ref[rows, :] = (u[rows] * s).astype(ya_ref.dtype)

    @pl.when(i == n_steps - 1)
    def _():
        last = slice((chunks_per_tile - 1) * SGU_CHUNK, chunks_per_tile * SGU_CHUNK)
        vrow_ref[...] = vn[last]


def _sgu(proj, w_eff, bias_eff, ln_g, ln_b, *, n_prompt_rows, width):
    nt = proj.shape[0]
    tr = ROW_TILE
    n_steps = nt // tr
    kern = functools.partial(
        _sgu_kernel, chunks_per_tile=tr // SGU_CHUNK,
        n_prompt_chunks=n_prompt_rows // SGU_CHUNK, n_steps=n_steps)
    return pl.pallas_call(
        kern,
        grid=(n_steps,),
        in_specs=[
            pl.BlockSpec((tr, width), lambda i: (i, 0)),
            pl.BlockSpec((tr, width), lambda i: (i, 1)),
            pl.BlockSpec(w_eff.shape, lambda i: (0, 0, 0, 0)),
            pl.BlockSpec(bias_eff.shape, lambda i: (0, 0, 0)),
            pl.BlockSpec((1, width), lambda i: (0, 0)),
            pl.BlockSpec((1, width), lambda i: (0, 0)),
        ],
        out_specs=[
            pl.BlockSpec((tr, width), lambda i: (i, 0)),
            pl.BlockSpec((SGU_CHUNK, width), lambda i: (0, 0)),
        ],
        out_shape=[
            jax.ShapeDtypeStruct((nt, width), BF16),
            jax.ShapeDtypeStruct((SGU_CHUNK, width), F32),
        ],
        compiler_params=_params(40),
        name="sgu",
    )(proj, proj, w_eff, bias_eff, ln_g, ln_b)


def _hgrn_gates(qp, z, lb):
    q = _silu(qp)
    e = jnp.exp(-jnp.abs(z))
    s_big = 1.0 / (1.0 + e)
    s_small = e * s_big
    pos = z >= 0.0
    one_m_lb = 1.0 - lb
    forget = lb + one_m_lb * jnp.where(pos, s_big, s_small)
    k = one_m_lb * jnp.where(pos, s_small, s_big)
    return q, forget, k


def _hgrn_prompt_kernel(q_ref, f_ref, i_ref, g_ref, lb_ref, ng_ref, yin_ref, y_ref, s_ref,
                        st_ref, lvl_ref, kpad, gpad, vpad, *, n_chunks):
    del yin_ref
    c = HGRN_CHUNK
    w = HGRN_BAND
    pad = HGRN_PAD
    nh = HGRN_HEADS_PER_STEP
    dk = HGRN_DK

    row = lax.broadcasted_iota(jnp.int32, (c, c), 0)
    col = lax.broadcasted_iota(jnp.int32, (c, c), 1)
    x = row ^ col
    lvl = jnp.where(x >= 64, 64, jnp.where(x >= 32, 32, jnp.where(x >= 16, 16, jnp.where(x >= 8, 8, 0))))
    lvl_ref[...] = jnp.where(row - col >= w, lvl, 0)
    st_ref[...] = jnp.zeros_like(st_ref)
    for ref in (kpad, gpad, vpad):
        ref[:, 0:pad, :] = jnp.zeros((nh, pad, dk), F32)
    r2 = lax.broadcasted_iota(jnp.int32, (2 * dk, 2 * dk), 0)
    c2 = lax.broadcasted_iota(jnp.int32, (2 * dk, 2 * dk), 1)
    pair_ones = jnp.where((r2 >= dk) == (c2 >= dk), 1.0, 0.0).astype(BF16)
    cur = pl.ds(pad, c)

    def head_chunk(hh, sl):
        cs = slice(hh * dk, (hh + 1) * dk)
        q, forget, k = _hgrn_gates(q_ref[sl, cs].astype(F32), f_ref[sl, cs].astype(F32),
                                   lb_ref[:, cs])
        v = i_ref[sl, cs].astype(F32)
        gcum = jnp.log2(forget)
        shift = 1
        while shift < c:
            gpad[hh, cur, :] = gcum
            gcum = gcum + gpad[hh, pl.ds(pad - shift, c), :]
            shift *= 2
        gpad[hh, cur, :] = gcum
        kpad[hh, cur, :] = k
        vpad[hh, cur, :] = v

        parts = []
        for d in range(w):
            if d == 0:
                p = q * k
            else:
                back = pl.ds(pad - d, c)
                p = q * kpad[hh, back, :] * jnp.exp2(gcum - gpad[hh, back, :])
            parts.append(p.astype(BF16))
        pairs = [jnp.concatenate(parts[2 * j:2 * j + 2], axis=1) for j in range(w // 2)]
        rs = jnp.dot(jnp.concatenate(pairs, axis=0), pair_ones, preferred_element_type=F32)
        o = rs[0:c, 0:dk] * v
        for d in range(1, w):
            j, half = divmod(d, 2)
            o = o + rs[j * c:(j + 1) * c, half * dk:(half + 1) * dk] * vpad[hh, pl.ds(pad - d, c), :]

        lv = lvl_ref[...]
        scores = jnp.zeros((c, c), F32)
        b = w
        while b < c:
            seg = gcum.reshape(c // (2 * b), 2 * b, dk)
            ref = jnp.broadcast_to(seg[:, b - 1:b, :], seg.shape).reshape(c, dk)
            a = jnp.exp2(-jnp.abs(gcum - ref))
            sc = lax.dot_general((q * a).astype(BF16), (k * a).astype(BF16),
                                 (((1,), (1,)), ((), ())), preferred_element_type=F32)
            scores = jnp.where(lv == b, sc, scores)
            b *= 2
        o = o + jnp.dot(scores.astype(BF16), v.astype(BF16), preferred_element_type=F32)

        st = st_ref[hh]
        qs = (q * jnp.exp2(gcum)).astype(BF16)
        o = o + lax.dot_general(qs, st.astype(BF16), (((1,), (1,)), ((), ())),
                                preferred_element_type=F32)
        glast = gcum[c - 1:c, :]
        ks = (k * jnp.exp2(glast - gcum)).astype(BF16)
        st_ref[hh] = st * jnp.exp2(glast) + jnp.dot(v.T.astype(BF16), ks, preferred_element_type=F32)

        ms = jnp.mean(o * o, axis=-1, keepdims=True)
        gate = ng_ref[:, cs] * _silu(g_ref[sl, cs].astype(F32))
        y_ref[sl, cs] = (o * lax.rsqrt(ms + LN_EPS) * gate).astype(y_ref.dtype)

    def chunk(ci, carry):
        sl = pl.ds(pl.multiple_of(ci * c, c), c)
        for hh in range(nh):
            head_chunk(hh, sl)
        return carry

    lax.fori_loop(0, n_chunks, chunk, 0)
    for hh in range(nh):
        s_ref[0, hh] = st_ref[hh].T


def _hgrn_prompt(proj, lb, ng, *, batch, seq, n_rows_out, col0):
    nh = HGRN_HEADS
    hs = HGRN_HEADS_PER_STEP
    wb = hs * HGRN_DK
    cb0 = col0 * HGRN_DK // wb
    nhb = nh // hs
    blk = lambda off: pl.BlockSpec((seq, wb), lambda b, h: (b, cb0 + off * nhb + h))
    vec = pl.BlockSpec((1, wb), lambda b, h: (0, h))
    kern = functools.partial(_hgrn_prompt_kernel, n_chunks=seq // HGRN_CHUNK)
    padded = pltpu.VMEM((hs, HGRN_PAD + HGRN_CHUNK, HGRN_DK), F32)
    return pl.pallas_call(
        kern,
        grid=(batch, nhb),
        in_specs=[blk(0), blk(1), blk(2), blk(3), vec, vec, pl.BlockSpec(memory_space=pl.ANY)],
        out_specs=[
            pl.BlockSpec((seq, wb), lambda b, h: (b, h)),
            pl.BlockSpec((1, hs, HGRN_DK, HGRN_DV), lambda b, h: (b, h, 0, 0)),
        ],
        out_shape=[
            jax.ShapeDtypeStruct((n_rows_out, nh * HGRN_DV), BF16),
            jax.ShapeDtypeStruct((batch, nh, HGRN_DK, HGRN_DV), F32),
        ],
        scratch_shapes=[
            pltpu.VMEM((hs, HGRN_DV, HGRN_DK), F32),
            pltpu.VMEM((HGRN_CHUNK, HGRN_CHUNK), jnp.int32),
            padded, padded, padded,
        ],
        input_output_aliases={6: 0},
        compiler_params=_params(32, 2),
        name="hgrn_prompt",
    )(proj, proj, proj, proj, lb, ng, jnp.zeros((n_rows_out, nh * HGRN_DV), BF16))


def _hgrn_sample_kernel(q_ref, f_ref, i_ref, g_ref, lb_ref, ng_ref, s_ref, yin_ref, sall_ref,
                        y_ref, so_ref, q_scr, f_scr, k_scr, v_scr, o_scr):
    del yin_ref, sall_ref
    nb = q_ref.shape[0]
    q, forget, k = _hgrn_gates(q_ref[...].astype(F32), f_ref[...].astype(F32), lb_ref[...])
    q_scr[...] = q
    f_scr[...] = forget
    k_scr[...] = k
    v_scr[...] = i_ref[...].astype(F32)
    sq = (HGRN_DK, HGRN_DV)

    def per_seq(n, carry):
        r = pl.ds(n, 1)
        qn, fn, kn, vn = q_scr[r, :], f_scr[r, :], k_scr[r, :], v_scr[r, :]
        for h in range(HGRN_HEADS):
            cs = slice(h * HGRN_DK, (h + 1) * HGRN_DK)
            col = lambda a: jnp.broadcast_to(a[:, cs], sq).T
            s_new = col(fn) * s_ref[n, h] + col(kn) * vn[:, h * HGRN_DV:(h + 1) * HGRN_DV]
            so_ref[n, h] = s_new
            o_scr[n, :, h * HGRN_DV:(h + 1) * HGRN_DV] = jnp.sum(col(qn) * s_new, axis=0, keepdims=True)
        return carry

    lax.fori_loop(0, nb, per_seq, 0)
    o = jnp.concatenate([o_scr[n] for n in range(nb)], axis=0)
    gate = ng_ref[...] * _silu(g_ref[...].astype(F32))
    outs = []
    for h in range(HGRN_HEADS):
        oh = o[:, h * HGRN_DV:(h + 1) * HGRN_DV]
        ms = jnp.mean(oh * oh, axis=-1, keepdims=True)
        outs.append(oh * lax.rsqrt(ms + LN_EPS))
    y_ref[...] = (jnp.concatenate(outs, axis=-1) * gate).astype(y_ref.dtype)


def _hgrn_sample(proj, lb, ng, state, y_b, state_out, layer, *, n_prompt_rows, n_seq, col0):
    nb = SAMPLE_SEQ_TILE
    r0 = n_prompt_rows // nb
    wk = HGRN_HEADS * HGRN_DK
    cb0 = col0 * HGRN_DK // wk
    blk = lambda off: pl.BlockSpec((nb, wk), lambda i: (r0 + i, cb0 + off))
    vec = pl.BlockSpec((1, wk), lambda i: (0, 0))
    st_spec = pl.BlockSpec((None, nb, HGRN_HEADS, HGRN_DK, HGRN_DV), lambda i: (layer, i, 0, 0, 0))
    anywhere = pl.BlockSpec(memory_space=pl.ANY)
    return pl.pallas_call(
        _hgrn_sample_kernel,
        grid=(n_seq // nb,),
        in_specs=[blk(0), blk(1), blk(2), blk(3), vec, vec, st_spec, anywhere, anywhere],
        out_specs=[pl.BlockSpec((nb, wk), lambda i: (r0 + i, 0)), st_spec],
        out_shape=[
            jax.ShapeDtypeStruct(y_b.shape, y_b.dtype),
            jax.ShapeDtypeStruct(state_out.shape, state_out.dtype),
        ],
        scratch_shapes=[pltpu.VMEM((nb, wk), F32)] * 4 + [pltpu.VMEM((nb, 1, wk), F32)],
        input_output_aliases={7: 0, 8: 1},
        compiler_params=_params(48),
        name="hgrn_sample",
    )(proj, proj, proj, proj, lb, ng, state, y_b, state_out)


def _merge_kernel(ya_ref, yb_ref, ga_ref, gb_ref, x_ref, woa_ref, wob_ref, wo_ref,
                  g_ref, b_ref, rw_ref, rb_ref, x1_ref, logit_ref, *, alpha):
    a = jnp.dot(ya_ref[...], woa_ref[...], preferred_element_type=F32)
    b = jnp.dot(yb_ref[...], wob_ref[...], preferred_element_type=F32)
    merged = (jax.nn.sigmoid(ga_ref[...].astype(F32)) * a
              + jax.nn.sigmoid(gb_ref[...].astype(F32)) * b)
    z = alpha * x_ref[...] + jnp.dot(merged.astype(BF16), wo_ref[...], preferred_element_type=F32)
    x1 = _layer_norm(z, g_ref[...], b_ref[...])
    x1_ref[...] = x1
    hi = x1.astype(BF16)
    mid = (x1 - hi.astype(F32)).astype(BF16)
    logit_ref[...] = jnp.dot(jnp.concatenate([hi, mid, hi], axis=1), rw_ref[...],
                             preferred_element_type=F32) + rb_ref[...]


def _merge(y_a, y_b, proj, x, woa, wob, wo, ln_g, ln_b, rw, rb, layer, *, alpha, gate_col_block):
    nt, d = x.shape
    wa = y_a.shape[1]
    tm = MERGE_ROW_TILE
    const = lambda shape: pl.BlockSpec(shape, lambda i: (0,) * len(shape),
                                       pipeline_mode=pl.Buffered(1))
    stacked = lambda a: pl.BlockSpec((None,) + a.shape[1:], lambda i: (layer, 0, 0),
                                     pipeline_mode=pl.Buffered(1))
    return pl.pallas_call(
        functools.partial(_merge_kernel, alpha=alpha),
        grid=(nt // tm,),
        in_specs=[
            pl.BlockSpec((tm, wa), lambda i: (i, 0)),
            pl.BlockSpec((tm, wa), lambda i: (i, 0)),
            pl.BlockSpec((tm, d), lambda i: (i, gate_col_block)),
            pl.BlockSpec((tm, d), lambda i: (i, gate_col_block + 1)),
            pl.BlockSpec((tm, d), lambda i: (i, 0)),
            stacked(woa), stacked(wob), stacked(wo),
            const(ln_g.shape), const(ln_b.shape), const(rw.shape), const(rb.shape),
        ],
        out_specs=[
            pl.BlockSpec((tm, d), lambda i: (i, 0)),
            pl.BlockSpec((tm, V7X_LANES), lambda i: (i, 0)),
        ],
        out_shape=[
            jax.ShapeDtypeStruct((nt, d), F32),
            jax.ShapeDtypeStruct((nt, V7X_LANES), F32),
        ],
        compiler_params=_params(56),
        name="merge_ln1",
    )(y_a, y_b, proj, proj, x, woa, wob, wo, ln_g, ln_b, rw, rb)


def _route_kernel(logit_ref, eid_ref, wcol_ref):
    tm = logit_ref.shape[0]
    lt = logit_ref[...].T[0:N_EXPERTS, :]
    e = jnp.exp(lt - jnp.max(lt, axis=0, keepdims=True))
    p = e / jnp.sum(e, axis=0, keepdims=True)
    pr = [p[j:j + 1, :] for j in range(N_EXPERTS)]

    best, gi = None, None
    for g in range(N_EXPERT_GROUPS):
        a, b, c, d = pr[4 * g:4 * g + 4]
        hi1, lo1 = jnp.maximum(a, b), jnp.minimum(a, b)
        hi2, lo2 = jnp.maximum(c, d), jnp.minimum(c, d)
        gs = jnp.maximum(hi1, hi2) + jnp.maximum(jnp.minimum(hi1, hi2), jnp.maximum(lo1, lo2))
        if g == 0:
            best, gi = gs, jnp.zeros((1, tm), jnp.int32)
        else:
            upd = gs > best
            best = jnp.where(upd, gs, best)
            gi = jnp.where(upd, g, gi)

    cand = []
    for j in range(EXPERTS_PER_GROUP):
        cj = pr[j]
        for g in range(1, N_EXPERT_GROUPS):
            cj = jnp.where(gi == g, pr[4 * g + j], cj)
        cand.append(cj)
    w1, i1 = cand[0], jnp.zeros((1, tm), jnp.int32)
    for j in range(1, EXPERTS_PER_GROUP):
        upd = cand[j] > w1
        w1 = jnp.where(upd, cand[j], w1)
        i1 = jnp.where(upd, j, i1)
    w2, i2 = jnp.full((1, tm), -1.0, F32), jnp.zeros((1, tm), jnp.int32)
    for j in range(EXPERTS_PER_GROUP):
        upd = (i1 != j) & (cand[j] > w2)
        w2 = jnp.where(upd, cand[j], w2)
        i2 = jnp.where(upd, j, i2)
    den = w1 + w2
    eid_ref[...] = jnp.concatenate([gi * EXPERTS_PER_GROUP + i1, gi * EXPERTS_PER_GROUP + i2], axis=0)
    wrows = jnp.concatenate([w1 / den, w2 / den, jnp.zeros((V7X_LANES - 2, tm), F32)], axis=0)
    wcol_ref[...] = wrows.T


def _route(logits):
    nt = logits.shape[0]
    tm = ROW_TILE
    return pl.pallas_call(
        _route_kernel,
        grid=(nt // tm,),
        in_specs=[pl.BlockSpec((tm, V7X_LANES), lambda i: (i, 0))],
        out_specs=[
            pl.BlockSpec((2, tm), lambda i: (0, i)),
            pl.BlockSpec((tm, V7X_LANES), lambda i: (i, 0)),
        ],
        out_shape=[
            jax.ShapeDtypeStruct((2, nt), jnp.int32),
            jax.ShapeDtypeStruct((nt, V7X_LANES), F32),
        ],
        compiler_params=_params(32),
        name="route",
    )(logits)


def _dispatch_plan(eid, *, n_rows_padded, tile):
    n_pairs = eid.size
    nt = eid.shape[1]
    e_flat = eid.reshape(n_pairs)
    onehot = (e_flat[:, None] == jnp.arange(N_EXPERTS, dtype=jnp.int32)[None, :]).astype(jnp.int32)
    csum = jnp.cumsum(onehot, axis=0)
    counts = csum[-1]
    rank = jnp.sum(onehot * csum, axis=1) - 1
    padded = ((counts + tile - 1) // tile) * tile
    ends = jnp.cumsum(padded)
    starts = ends - padded
    dest = starts[e_flat] + rank
    tok = jnp.arange(n_pairs, dtype=jnp.int32) % nt
    src = jnp.zeros((n_rows_padded,), jnp.int32).at[dest].set(tok, unique_indices=True)
    n_tiles = n_rows_padded // tile
    tile_start = jnp.arange(n_tiles, dtype=jnp.int32) * tile
    tile_expert = jnp.minimum(
        jnp.sum((tile_start[:, None] >= ends[None, :]).astype(jnp.int32), axis=1), N_EXPERTS - 1)
    n_valid = (ends[-1] // tile).astype(jnp.int32).reshape(1)
    last_expert = tile_expert[jnp.maximum(n_valid[0] - 1, 0)]
    tile_expert = jnp.where(jnp.arange(n_tiles) < n_valid[0], tile_expert, last_expert).astype(jnp.int32)
    return src, dest.reshape(2, nt).astype(jnp.int32), tile_expert, n_valid


def _row_copy(src_hbm, row, dst_vmem, r, sem):
    return pltpu.make_async_copy(src_hbm.at[pl.ds(row, 1), :], dst_vmem.at[pl.ds(r, 1), :], sem)


def _expert_up_kernel(te_ref, nvalid_ref, src_ref, src_next_ref, x_hbm, wg_ref, wu_ref, h_ref,
                      xbuf, sem):
    del te_ref
    t = pl.program_id(0)
    nv = nvalid_ref[0]
    tm = xbuf.shape[1]
    slot = lax.rem(t, 2)

    def issue(idx_ref, s):
        def body(g, carry):
            for j in range(DMA_UNROLL):
                r = g * DMA_UNROLL + j
                _row_copy(x_hbm, idx_ref[0, 0, r], xbuf.at[s], r, sem.at[s]).start(priority=j % 2)
            return carry
        lax.fori_loop(0, tm // DMA_UNROLL, body, 0)

    def drain(s):
        def body(g, carry):
            for j in range(DMA_UNROLL):
                _row_copy(x_hbm, 0, xbuf.at[s], g * DMA_UNROLL + j, sem.at[s]).wait()
            return carry
        lax.fori_loop(0, tm // DMA_UNROLL, body, 0)

    @pl.when((t == 0) & (nv > 0))
    def _():
        issue(src_ref, 0)

    @pl.when(t + 1 < nv)
    def _():
        issue(src_next_ref, 1 - slot)

    @pl.when(t < nv)
    def _():
        drain(slot)
        x = xbuf[slot].astype(BF16).astype(F32)
        g = jnp.dot(x, wg_ref[...], preferred_element_type=F32)
        u = jnp.dot(x, wu_ref[...], preferred_element_type=F32)
        h_ref[...] = (_silu(g) * u).astype(h_ref.dtype)

    @pl.when(t >= nv)
    def _():
        h_ref[...] = jnp.zeros_like(h_ref)


def _expert_down_kernel(te_ref, nvalid_ref, h_ref, wd_ref, o_ref):
    del te_ref
    t = pl.program_id(0)

    @pl.when(t < nvalid_ref[0])
    def _():
        o_ref[...] = jnp.dot(h_ref[...].astype(F32), wd_ref[...], preferred_element_type=F32)

    @pl.when(t >= nvalid_ref[0])
    def _():
        o_ref[...] = jnp.zeros_like(o_ref)


def _experts(x1, src, w_gate, w_up, w_down, tile_expert, n_valid, layer, *, tile):
    d = x1.shape[1]
    dff = w_gate.shape[-1]
    r = src.shape[0]
    n_tiles = r // tile
    src3 = src.reshape(n_tiles, 1, tile)
    wspec = lambda a, b: pl.BlockSpec((None, None, a, b), lambda t, te, nv: (layer, te[t], 0, 0))
    h = pl.pallas_call(
        _expert_up_kernel,
        grid_spec=pltpu.PrefetchScalarGridSpec(
            num_scalar_prefetch=2,
            grid=(n_tiles,),
            in_specs=[
                pl.BlockSpec((1, 1, tile), lambda t, te, nv: (t, 0, 0), memory_space=pltpu.SMEM),
                pl.BlockSpec((1, 1, tile), lambda t, te, nv: (jnp.minimum(t + 1, n_tiles - 1), 0, 0),
                             memory_space=pltpu.SMEM),
                pl.BlockSpec(memory_space=pl.ANY),
                wspec(d, dff), wspec(d, dff),
            ],
            out_specs=pl.BlockSpec((tile, dff), lambda t, te, nv: (t, 0)),
            scratch_shapes=[pltpu.VMEM((2, tile, d), F32), pltpu.SemaphoreType.DMA((2,))],
        ),
        out_shape=jax.ShapeDtypeStruct((r, dff), BF16),
        compiler_params=_params(56),
        name="moe_up",
    )(tile_expert, n_valid, src3, src3, x1, w_gate, w_up)
    return pl.pallas_call(
        _expert_down_kernel,
        grid_spec=pltpu.PrefetchScalarGridSpec(
            num_scalar_prefetch=2,
            grid=(n_tiles,),
            in_specs=[
                pl.BlockSpec((tile, dff), lambda t, te, nv: (t, 0)),
                wspec(dff, d),
            ],
            out_specs=pl.BlockSpec((tile, d), lambda t, te, nv: (t, 0)),
        ),
        out_shape=jax.ShapeDtypeStruct((r, d), F32),
        compiler_params=_params(40),
        name="moe_down",
    )(tile_expert, n_valid, h, w_down)


def _combine_kernel(pos_ref, x_ref, wcol_ref, g_ref, b_ref, ys_hbm, x2_ref, x2b_ref,
                    buf0, buf1, sem, *, alpha):
    tm = x_ref.shape[0]
    part = tm // COMBINE_PARTS

    def copies(p, r, first_row, second_row):
        return (_row_copy(ys_hbm, first_row, buf0, r, sem.at[p]),
                _row_copy(ys_hbm, second_row, buf1, r, sem.at[p]))

    def issue(p):
        def body(g, carry):
            for j in range(DMA_UNROLL):
                r = p * part + g * DMA_UNROLL + j
                for cp in copies(p, r, pos_ref[0, 0, r], pos_ref[0, 1, r]):
                    cp.start(priority=j % 2)
            return carry
        lax.fori_loop(0, part // DMA_UNROLL, body, 0)

    def drain(p):
        def body(g, carry):
            for j in range(DMA_UNROLL):
                for cp in copies(p, p * part + g * DMA_UNROLL + j, 0, 0):
                    cp.wait()
            return carry
        lax.fori_loop(0, part // DMA_UNROLL, body, 0)

    for p in range(COMBINE_PARTS):
        issue(p)
    for p in range(COMBINE_PARTS):
        drain(p)
        rows = slice(p * part, (p + 1) * part)
        wc = wcol_ref[rows, :]
        z = alpha * x_ref[rows, :] + wc[:, 0:1] * buf0[rows, :] + wc[:, 1:2] * buf1[rows, :]
        x2 = _layer_norm(z, g_ref[...], b_ref[...])
        x2_ref[rows, :] = x2
        x2b_ref[rows, :] = x2.astype(BF16)


def _combine(pos, x1, wcol, ln_g, ln_b, ys, *, alpha):
    nt, d = x1.shape
    tm = COMBINE_ROW_TILE
    n_steps = nt // tm
    pos_t = pos.reshape(2, n_steps, tm).transpose(1, 0, 2)
    return pl.pallas_call(
        functools.partial(_combine_kernel, alpha=alpha),
        grid=(n_steps,),
        in_specs=[
            pl.BlockSpec((1, 2, tm), lambda i: (i, 0, 0), memory_space=pltpu.SMEM),
            pl.BlockSpec((tm, d), lambda i: (i, 0)),
            pl.BlockSpec((tm, V7X_LANES), lambda i: (i, 0)),
            pl.BlockSpec((1, d), lambda i: (0, 0)),
            pl.BlockSpec((1, d), lambda i: (0, 0)),
            pl.BlockSpec(memory_space=pl.ANY),
        ],
        out_specs=[pl.BlockSpec((tm, d), lambda i: (i, 0)), pl.BlockSpec((tm, d), lambda i: (i, 0))],
        out_shape=[jax.ShapeDtypeStruct((nt, d), F32), jax.ShapeDtypeStruct((nt, d), BF16)],
        scratch_shapes=[pltpu.VMEM((tm, d), F32), pltpu.VMEM((tm, d), F32),
                        pltpu.SemaphoreType.DMA((COMBINE_PARTS,))],
        compiler_params=_params(56),
        name="moe_combine_ln2",
    )(pos_t, x1, wcol, ln_g, ln_b, ys)


def kernel(x_prompt, x_sample, state_hgrn, w_in, sgu_ln_g, sgu_ln_b, w_s, b_s, lb_logits,
           hgrn_norm_g, w_out_a, w_out_b, w_o, ln1_g, ln1_b, router_w, router_b, w_gate, w_up,
           w_down, ln2_g, ln2_b):
    batch, seq, d = x_prompt.shape
    n_seq = x_sample.shape[0]
    depth = w_in.shape[0]
    a_width = sgu_ln_g.shape[1]
    n_prompt = batch * seq
    nt = n_prompt + n_seq
    alpha = (2.0 * depth) ** 0.25
    assert x_sample.shape[1] == 1 and n_seq == SGU_CHUNK and nt % ROW_TILE == 0
    assert a_width == SGU_GROUPS * SGU_CHUNK and d == 2 * a_width

    p = jax.nn.softmax(lb_logits.astype(F32), axis=0)
    lower_bounds = jnp.cumsum(p, axis=0) - p[0:1]

    n_rows_padded = 2 * nt + N_EXPERTS * MOE_ROW_TILE
    n_rows_padded = -(-n_rows_padded // MOE_ROW_TILE) * MOE_ROW_TILE

    woa_b, wob_b, wo_b = (a.astype(BF16) for a in (w_out_a, w_out_b, w_o))
    rw_pad = jnp.zeros((d, V7X_LANES), F32).at[:, :N_EXPERTS].set(router_w.astype(F32))
    rw_hi = rw_pad.astype(BF16)
    rw_mid = (rw_pad - rw_hi.astype(F32)).astype(BF16)
    rw3 = jnp.concatenate([rw_hi, rw_hi, rw_mid], axis=0)
    rb_pad = jnp.zeros((1, V7X_LANES), F32).at[0, :N_EXPERTS].set(router_b.astype(F32))
    s_sample = jnp.zeros(state_hgrn.shape, F32)

    causal = jnp.tril(jnp.ones((SGU_CHUNK, SGU_CHUNK), bool))
    eye = jnp.eye(SGU_CHUNK, dtype=F32)
    gd = a_width // SGU_GROUPS

    x = jnp.concatenate([x_prompt.reshape(n_prompt, d), x_sample.reshape(n_seq, d)], axis=0)
    x_b = x.astype(BF16)
    s_prompt, v_sample = [], []
    for l in range(depth):
        w_eff = jnp.stack([
            jnp.where(causal[None], w_s[l], 0.0),
            w_s[l][:, 0:1, 0:1] * eye[None],
        ]).astype(BF16)
        bias_eff = jnp.stack([
            jnp.repeat(b_s[l].T, gd, axis=1),
            jnp.broadcast_to(jnp.repeat(b_s[l][:, 0], gd)[None, :], (SGU_CHUNK, a_width)),
        ]).astype(F32)
        lb_l = lower_bounds[l].reshape(1, -1)
        ng_l = hgrn_norm_g[l].astype(F32).reshape(1, -1)

        proj = _matmul(x_b, w_in, l, tm=PROJ_ROW_TILE, tn=PROJ_COL_TILE, out_dtype=BF16)
        y_a, v_rows = _sgu(proj, w_eff, bias_eff, sgu_ln_g[l].reshape(1, -1).astype(F32),
                           sgu_ln_b[l].reshape(1, -1).astype(F32),
                           n_prompt_rows=n_prompt, width=a_width)
        q_col0 = 2 * a_width // HGRN_DK
        y_b, s_p = _hgrn_prompt(proj, lb_l, ng_l, batch=batch, seq=seq, n_rows_out=nt, col0=q_col0)
        y_b, s_sample = _hgrn_sample(proj, lb_l, ng_l, state_hgrn, y_b, s_sample, l,
                                     n_prompt_rows=n_prompt, n_seq=n_seq, col0=q_col0)
        gate_col_block = (2 * a_width + 4 * HGRN_HEADS * HGRN_DK) // d
        x1, logits = _merge(y_a, y_b, proj, x, woa_b, wob_b, wo_b,
                            ln1_g[l].reshape(1, -1).astype(F32), ln1_b[l].reshape(1, -1).astype(F32),
                            rw3, rb_pad, l, alpha=alpha, gate_col_block=gate_col_block)
        eid, wcol = _route(logits)
        src, pos, tile_expert, n_valid = _dispatch_plan(
            eid, n_rows_padded=n_rows_padded, tile=MOE_ROW_TILE)
        ys = _experts(x1, src, w_gate, w_up, w_down, tile_expert, n_valid, l, tile=MOE_ROW_TILE)
        x, x_b = _combine(pos, x1, wcol, ln2_g[l].reshape(1, -1).astype(F32),
                          ln2_b[l].reshape(1, -1).astype(F32), ys, alpha=alpha)
        s_prompt.append(s_p)
        v_sample.append(v_rows.reshape(n_seq, 1, a_width))

    y_prompt = x[:n_prompt].reshape(batch, seq, d)
    y_sample = x[n_prompt:].reshape(n_seq, 1, d)
    return (y_prompt, y_sample, jnp.stack(s_prompt), s_sample, jnp.stack(v_sample))
```

```python
import functools

import jax
import jax.numpy as jnp
from jax import lax
from jax.experimental import pallas as pl
from jax.experimental.pallas import tpu as pltpu

F32 = jnp.float32
BF16 = jnp.bfloat16

SGU_GROUPS = 8
SGU_CHUNK = 128
HGRN_HEADS = 8
HGRN_DK = 128
HGRN_DV = 128
N_EXPERTS = 16
N_EXPERT_GROUPS = 4
EXPERTS_PER_GROUP = N_EXPERTS // N_EXPERT_GROUPS
LN_EPS = 1e-5

V7X_LANES = 128
V7X_SUBLANES = 8
V7X_VMEM_BYTES = 64 * 1024 * 1024

ROW_TILE = 640
PROJ_ROW_TILE = 832
MERGE_ROW_TILE = 416
COMBINE_PARTS = 2
DMA_UNROLL = 8
PROJ_COL_TILE = 1024
HGRN_CHUNK = 128
HGRN_BAND = 8
HGRN_PAD = 64
HGRN_HEADS_PER_STEP = 4
SAMPLE_SEQ_TILE = 16
MOE_ROW_TILE = 256
COMBINE_ROW_TILE = 640


def _params(vmem_mb, n_axes=1):
    return pltpu.CompilerParams(
        dimension_semantics=("arbitrary",) * n_axes,
        vmem_limit_bytes=vmem_mb * 1024 * 1024,
    )


def _layer_norm(x, g, b):
    mu = jnp.mean(x, axis=-1, keepdims=True)
    xc = x - mu
    var = jnp.mean(xc * xc, axis=-1, keepdims=True)
    return xc * lax.rsqrt(var + LN_EPS) * g + b


def _silu(x):
    return x * jax.nn.sigmoid(x)


def _matmul_kernel(x_ref, w_ref, o_ref, wb_ref):
    @pl.when(pl.program_id(1) == 0)
    def _():
        wb_ref[...] = w_ref[...].astype(wb_ref.dtype)

    o_ref[...] = jnp.dot(x_ref[...], wb_ref[...], preferred_element_type=F32).astype(o_ref.dtype)


def _matmul(x, w, layer, *, tm, tn, out_dtype):
    m, k = x.shape
    n = w.shape[-1]
    return pl.pallas_call(
        _matmul_kernel,
        grid=(n // tn, m // tm),
        in_specs=[
            pl.BlockSpec((tm, k), lambda j, i: (i, 0)),
            pl.BlockSpec((None, k, tn), lambda j, i: (layer, 0, j)),
        ],
        out_specs=pl.BlockSpec((tm, tn), lambda j, i: (i, j)),
        out_shape=jax.ShapeDtypeStruct((m, n), out_dtype),
        scratch_shapes=[pltpu.VMEM((k, tn), BF16)],
        compiler_params=_params(48, 2),
        name="proj_matmul",
    )(x, w)


def _sgu_kernel(u_ref, v_ref, w_ref, bias_ref, g_ref, b_ref, ya_ref, vrow_ref, *,
                chunks_per_tile, n_prompt_chunks, n_steps):
    i = pl.program_id(0)
    u = jax.nn.gelu(u_ref[...].astype(F32))
    vn = _layer_norm(jax.nn.gelu(v_ref[...].astype(F32)), g_ref[...], b_ref[...])
    gd = vn.shape[-1] // SGU_GROUPS
    for c in range(chunks_per_tile):
        rows = slice(c * SGU_CHUNK, (c + 1) * SGU_CHUNK)
        variant = (i * chunks_per_tile + c >= n_prompt_chunks).astype(jnp.int32)
        w = w_ref[variant]
        vc = vn[rows].astype(BF16)
        parts = [
            jnp.dot(w[g], vc[:, g * gd:(g + 1) * gd], preferred_element_type=F32)
            for g in range(SGU_GROUPS)
        ]
        s = jnp.concatenate(parts, axis=-1) + bias_ref[variant]
        ya_ref[rows, :] = (u[rows] * s).astype(ya_ref.dtype)

    @pl.when(i == n_steps - 1)
    def _():
        last = slice((chunks_per_tile - 1) * SGU_CHUNK, chunks_per_tile * SGU_CHUNK)
        vrow_ref[...] = vn[last]


def _sgu(proj, w_eff, bias_eff, ln_g, ln_b, *, n_prompt_rows, width):
    nt = proj.shape[0]
    tr = ROW_TILE
    n_steps = nt // tr
    kern = functools.partial(
        _sgu_kernel, chunks_per_tile=tr // SGU_CHUNK,
        n_prompt_chunks=n_prompt_rows // SGU_CHUNK, n_steps=n_steps)
    return pl.pallas_call(
        kern,
        grid=(n_steps,),
        in_specs=[
            pl.BlockSpec((tr, width), lambda i: (i, 0)),
            pl.BlockSpec((tr, width), lambda i: (i, 1)),
            pl.BlockSpec(w_eff.shape, lambda i: (0, 0, 0, 0)),
            pl.BlockSpec(bias_eff.shape, lambda i: (0, 0, 0)),
            pl.BlockSpec((1, width), lambda i: (0, 0)),
            pl.BlockSpec((1, width), lambda i: (0, 0)),
        ],
        out_specs=[
            pl.BlockSpec((tr, width), lambda i: (i, 0)),
            pl.BlockSpec((SGU_CHUNK, width), lambda i: (0, 0)),
        ],
        out_shape=[
            jax.ShapeDtypeStruct((nt, width), BF16),
            jax.ShapeDtypeStruct((SGU_CHUNK, width), F32),
        ],
        compiler_params=_params(40),
        name="sgu",
    )(proj, proj, w_eff, bias_eff, ln_g, ln_b)


def _hgrn_gates(qp, z, lb):
    q = _silu(qp)
    e = jnp.exp(-jnp.abs(z))
    s_big = 1.0 / (1.0 + e)
    s_small = e * s_big
    pos = z >= 0.0
    one_m_lb = 1.0 - lb
    forget = lb + one_m_lb * jnp.where(pos, s_big, s_small)
    k = one_m_lb * jnp.where(pos, s_small, s_big)
    return q, forget, k


def _hgrn_prompt_kernel(q_ref, f_ref, i_ref, g_ref, lb_ref, ng_ref, yin_ref, y_ref, s_ref,
                        st_ref, lvl_ref, kpad, gpad, vpad, *, n_chunks):
    del yin_ref
    c = HGRN_CHUNK
    w = HGRN_BAND
    pad = HGRN_PAD
    nh = HGRN_HEADS_PER_STEP
    dk = HGRN_DK

    row = lax.broadcasted_iota(jnp.int32, (c, c), 0)
    col = lax.broadcasted_iota(jnp.int32, (c, c), 1)
    x = row ^ col
    lvl = jnp.where(x >= 64, 64, jnp.where(x >= 32, 32, jnp.where(x >= 16, 16, jnp.where(x >= 8, 8, 0))))
    lvl_ref[...] = jnp.where(row - col >= w, lvl, 0)
    st_ref[...] = jnp.zeros_like(st_ref)
    for ref in (kpad, gpad, vpad):
        ref[:, 0:pad, :] = jnp.zeros((nh, pad, dk), F32)
    r2 = lax.broadcasted_iota(jnp.int32, (2 * dk, 2 * dk), 0)
    c2 = lax.broadcasted_iota(jnp.int32, (2 * dk, 2 * dk), 1)
    pair_ones = jnp.where((r2 >= dk) == (c2 >= dk), 1.0, 0.0).astype(BF16)
    cur = pl.ds(pad, c)

    def head_chunk(hh, sl):
        cs = slice(hh * dk, (hh + 1) * dk)
        q, forget, k = _hgrn_gates(q_ref[sl, cs].astype(F32), f_ref[sl, cs].astype(F32),
                                   lb_ref[:, cs])
        v = i_ref[sl, cs].astype(F32)
        gcum = jnp.log2(forget)
        shift = 1
        while shift < c:
            gpad[hh, cur, :] = gcum
            gcum = gcum + gpad[hh, pl.ds(pad - shift, c), :]
            shift *= 2
        gpad[hh, cur, :] = gcum
        kpad[hh, cur, :] = k
        vpad[hh, cur, :] = v

        parts = []
        for d in range(w):
            if d == 0:
                p = q * k
            else:
                back = pl.ds(pad - d, c)
                p = q * kpad[hh, back, :] * jnp.exp2(gcum - gpad[hh, back, :])
            parts.append(p.astype(BF16))
        pairs = [jnp.concatenate(parts[2 * j:2 * j + 2], axis=1) for j in range(w // 2)]
        rs = jnp.dot(jnp.concatenate(pairs, axis=0), pair_ones, preferred_element_type=F32)
        o = rs[0:c, 0:dk] * v
        for d in range(1, w):
            j, half = divmod(d, 2)
            o = o + rs[j * c:(j + 1) * c, half * dk:(half + 1) * dk] * vpad[hh, pl.ds(pad - d, c), :]

        lv = lvl_ref[...]
        scores = jnp.zeros((c, c), F32)
        b = w
        while b < c:
            seg = gcum.reshape(c // (2 * b), 2 * b, dk)
            ref = jnp.broadcast_to(seg[:, b - 1:b, :], seg.shape).reshape(c, dk)
            a = jnp.exp2(-jnp.abs(gcum - ref))
            sc = lax.dot_general((q * a).astype(BF16), (k * a).astype(BF16),
                                 (((1,), (1,)), ((), ())), preferred_element_type=F32)
            scores = jnp.where(lv == b, sc, scores)
            b *= 2
        o = o + jnp.dot(scores.astype(BF16), v.astype(BF16), preferred_element_type=F32)

        st = st_ref[hh]
        qs = (q * jnp.exp2(gcum)).astype(BF16)
        o = o + lax.dot_general(qs, st.astype(BF16), (((1,), (1,)), ((), ())),
                                preferred_element_type=F32)
        glast = gcum[c - 1:c, :]
        ks = (k * jnp.exp2(glast - gcum)).astype(BF16)
        st_ref[hh] = st * jnp.exp2(glast) + jnp.dot(v.T.astype(BF16), ks, preferred_element_type=F32)

        ms = jnp.mean(o * o, axis=-1, keepdims=True)
        gate = ng_ref[:, cs] * _silu(g_ref[sl, cs].astype(F32))
        y_ref[sl, cs] = (o * lax.rsqrt(ms + LN_EPS) * gate).astype(y_ref.dtype)

    def chunk(ci, carry):
        sl = pl.ds(pl.multiple_of(ci * c, c), c)
        for hh in range(nh):
            head_chunk(hh, sl)
        return carry

    lax.fori_loop(0, n_chunks, chunk, 0)
    for hh in range(nh):
        s_ref[0, hh] = st_ref[hh].T


def _hgrn_prompt(proj, lb, ng, *, batch, seq, n_rows_out, col0):
    nh = HGRN_HEADS
    hs = HGRN_HEADS_PER_STEP
    wb = hs * HGRN_DK
    cb0 = col0 * HGRN_DK // wb
    nhb = nh // hs
    blk = lambda off: pl.BlockSpec((seq, wb), lambda b, h: (b, cb0 + off * nhb + h))
    vec = pl.BlockSpec((1, wb), lambda b, h: (0, h))
    kern = functools.partial(_hgrn_prompt_kernel, n_chunks=seq // HGRN_CHUNK)
    padded = pltpu.VMEM((hs, HGRN_PAD + HGRN_CHUNK, HGRN_DK), F32)
    return pl.pallas_call(
        kern,
        grid=(batch, nhb),
        in_specs=[blk(0), blk(1), blk(2), blk(3), vec, vec, pl.BlockSpec(memory_space=pl.ANY)],
        out_specs=[
            pl.BlockSpec((seq, wb), lambda b, h: (b, h)),
            pl.BlockSpec((1, hs, HGRN_DK, HGRN_DV), lambda b, h: (b, h, 0, 0)),
        ],
        out_shape=[
            jax.ShapeDtypeStruct((n_rows_out, nh * HGRN_DV), BF16),
            jax.ShapeDtypeStruct((batch, nh, HGRN_DK, HGRN_DV), F32),
        ],
        scratch_shapes=[
            pltpu.VMEM((hs, HGRN_DV, HGRN_DK), F32),
            pltpu.VMEM((HGRN_CHUNK, HGRN_CHUNK), jnp.int32),
            padded, padded, padded,
        ],
        input_output_aliases={6: 0},
        compiler_params=_params(32, 2),
        name="hgrn_prompt",
    )(proj, proj, proj, proj, lb, ng, jnp.zeros((n_rows_out, nh * HGRN_DV), BF16))


def _hgrn_sample_kernel(q_ref, f_ref, i_ref, g_ref, lb_ref, ng_ref, s_ref, yin_ref, sall_ref,
                        y_ref, so_ref, q_scr, f_scr, k_scr, v_scr, o_scr):
    del yin_ref, sall_ref
    nb = q_ref.shape[0]
    q, forget, k = _hgrn_gates(q_ref[...].astype(F32), f_ref[...].astype(F32), lb_ref[...])
    q_scr[...] = q
    f_scr[...] = forget
    k_scr[...] = k
    v_scr[...] = i_ref[...].astype(F32)
    sq = (HGRN_DK, HGRN_DV)

    def per_seq(n, carry):
        r = pl.ds(n, 1)
        qn, fn, kn, vn = q_scr[r, :], f_scr[r, :], k_scr[r, :], v_scr[r, :]
        for h in range(HGRN_HEADS):
            cs = slice(h * HGRN_DK, (h + 1) * HGRN_DK)
            col = lambda a: jnp.broadcast_to(a[:, cs], sq).T
            s_new = col(fn) * s_ref[n, h] + col(kn) * vn[:, h * HGRN_DV:(h + 1) * HGRN_DV]
            so_ref[n, h] = s_new
            o_scr[n, :, h * HGRN_DV:(h + 1) * HGRN_DV] = jnp.sum(col(qn) * s_new, axis=0, keepdims=True)
        return carry

    lax.fori_loop(0, nb, per_seq, 0)
    o = jnp.concatenate([o_scr[n] for n in range(nb)], axis=0)
    gate = ng_ref[...] * _silu(g_ref[...].astype(F32))
    outs = []
    for h in range(HGRN_HEADS):
        oh = o[:, h * HGRN_DV:(h + 1) * HGRN_DV]
        ms = jnp.mean(oh * oh, axis=-1, keepdims=True)
        outs.append(oh * lax.rsqrt(ms + LN_EPS))
    y_ref[...] = (jnp.concatenate(outs, axis=-1) * gate).astype(y_ref.dtype)


def _hgrn_sample(proj, lb, ng, state, y_b, state_out, layer, *, n_prompt_rows, n_seq, col0):
    nb = SAMPLE_SEQ_TILE
    r0 = n_prompt_rows // nb
    wk = HGRN_HEADS * HGRN_DK
    cb0 = col0 * HGRN_DK // wk
    blk = lambda off: pl.BlockSpec((nb, wk), lambda i: (r0 + i, cb0 + off))
    vec = pl.BlockSpec((1, wk), lambda i: (0, 0))
    st_spec = pl.BlockSpec((None, nb, HGRN_HEADS, HGRN_DK, HGRN_DV), lambda i: (layer, i, 0, 0, 0))
    anywhere = pl.BlockSpec(memory_space=pl.ANY)
    return pl.pallas_call(
        _hgrn_sample_kernel,
        grid=(n_seq // nb,),
        in_specs=[blk(0), blk(1), blk(2), blk(3), vec, vec, st_spec, anywhere, anywhere],
        out_specs=[pl.BlockSpec((nb, wk), lambda i: (r0 + i, 0)), st_spec],
        out_shape=[
            jax.ShapeDtypeStruct(y_b.shape, y_b.dtype),
            jax.ShapeDtypeStruct(state_out.shape, state_out.dtype),
        ],
        scratch_shapes=[pltpu.VMEM((nb, wk), F32)] * 4 + [pltpu.VMEM((nb, 1, wk), F32)],
        input_output_aliases={7: 0, 8: 1},
        compiler_params=_params(48),
        name="hgrn_sample",
    )(proj, proj, proj, proj, lb, ng, state, y_b, state_out)


def _store_token_major(ref, val):
    rows = val.shape[0]
    nblk = val.shape[1] // V7X_LANES
    for j in range(nblk):
        ref[pl.ds(j, rows, stride=nblk), :] = val[:, j * V7X_LANES:(j + 1) * V7X_LANES]


def _load_token_major(ref, first_token, rows, nblk):
    return jnp.concatenate(
        [ref[pl.ds(first_token * nblk + j, rows, stride=nblk), :] for j in range(nblk)], axis=1)


def _merge_kernel(ya_ref, yb_ref, ga_ref, gb_ref, x_ref, woa_ref, wob_ref, wo_ref,
                  g_ref, b_ref, rw_ref, rb_ref, x1_ref, x1t_ref, logit_ref, *, alpha):
    a = jnp.dot(ya_ref[...], woa_ref[...], preferred_element_type=F32)
    b = jnp.dot(yb_ref[...], wob_ref[...], preferred_element_type=F32)
    merged = (jax.nn.sigmoid(ga_ref[...].astype(F32)) * a
              + jax.nn.sigmoid(gb_ref[...].astype(F32)) * b)
    z = alpha * x_ref[...] + jnp.dot(merged.astype(BF16), wo_ref[...], preferred_element_type=F32)
    x1 = _layer_norm(z, g_ref[...], b_ref[...])
    x1_ref[...] = x1
    _store_token_major(x1t_ref, x1)
    hi = x1.astype(BF16)
    mid = (x1 - hi.astype(F32)).astype(BF16)
    logit_ref[...] = jnp.dot(jnp.concatenate([hi, mid, hi], axis=1), rw_ref[...],
                             preferred_element_type=F32) + rb_ref[...]


def _merge(y_a, y_b, proj, x, woa, wob, wo, ln_g, ln_b, rw, rb, layer, *, alpha, gate_col_block):
    nt, d = x.shape
    wa = y_a.shape[1]
    tm = MERGE_ROW_TILE
    const = lambda shape: pl.BlockSpec(shape, lambda i: (0,) * len(shape),
                                       pipeline_mode=pl.Buffered(1))
    stacked = lambda a: pl.BlockSpec((None,) + a.shape[1:], lambda i: (layer, 0, 0),
                                     pipeline_mode=pl.Buffered(1))
    return pl.pallas_call(
        functools.partial(_merge_kernel, alpha=alpha),
        grid=(nt // tm,),
        in_specs=[
            pl.BlockSpec((tm, wa), lambda i: (i, 0)),
            pl.BlockSpec((tm, wa), lambda i: (i, 0)),
            pl.BlockSpec((tm, d), lambda i: (i, gate_col_block)),
            pl.BlockSpec((tm, d), lambda i: (i, gate_col_block + 1)),
            pl.BlockSpec((tm, d), lambda i: (i, 0)),
            stacked(woa), stacked(wob), stacked(wo),
            const(ln_g.shape), const(ln_b.shape), const(rw.shape), const(rb.shape),
        ],
        out_specs=[
            pl.BlockSpec((tm, d), lambda i: (i, 0)),
            pl.BlockSpec((tm * (d // V7X_LANES), V7X_LANES), lambda i: (i, 0)),
            pl.BlockSpec((tm, V7X_LANES), lambda i: (i, 0)),
        ],
        out_shape=[
            jax.ShapeDtypeStruct((nt, d), F32),
            jax.ShapeDtypeStruct((nt * (d // V7X_LANES), V7X_LANES), F32),
            jax.ShapeDtypeStruct((nt, V7X_LANES), F32),
        ],
        compiler_params=_params(56),
        name="merge_ln1",
    )(y_a, y_b, proj, proj, x, woa, wob, wo, ln_g, ln_b, rw, rb)


def _route_kernel(logit_ref, eid_ref, wcol_ref):
    tm = logit_ref.shape[0]
    lt = logit_ref[...].T[0:N_EXPERTS, :]
    e = jnp.exp(lt - jnp.max(lt, axis=0, keepdims=True))
    p = e / jnp.sum(e, axis=0, keepdims=True)
    pr = [p[j:j + 1, :] for j in range(N_EXPERTS)]

    best, gi = None, None
    for g in range(N_EXPERT_GROUPS):
        a, b, c, d = pr[4 * g:4 * g + 4]
        hi1, lo1 = jnp.maximum(a, b), jnp.minimum(a, b)
        hi2, lo2 = jnp.maximum(c, d), jnp.minimum(c, d)
        gs = jnp.maximum(hi1, hi2) + jnp.maximum(jnp.minimum(hi1, hi2), jnp.maximum(lo1, lo2))
        if g == 0:
            best, gi = gs, jnp.zeros((1, tm), jnp.int32)
        else:
            upd = gs > best
            best = jnp.where(upd, gs, best)
            gi = jnp.where(upd, g, gi)

    cand = []
    for j in range(EXPERTS_PER_GROUP):
        cj = pr[j]
        for g in range(1, N_EXPERT_GROUPS):
            cj = jnp.where(gi == g, pr[4 * g + j], cj)
        cand.append(cj)
    w1, i1 = cand[0], jnp.zeros((1, tm), jnp.int32)
    for j in range(1, EXPERTS_PER_GROUP):
        upd = cand[j] > w1
        w1 = jnp.where(upd, cand[j], w1)
        i1 = jnp.where(upd, j, i1)
    w2, i2 = jnp.full((1, tm), -1.0, F32), jnp.zeros((1, tm), jnp.int32)
    for j in range(EXPERTS_PER_GROUP):
        upd = (i1 != j) & (cand[j] > w2)
        w2 = jnp.where(upd, cand[j], w2)
        i2 = jnp.where(upd, j, i2)
    den = w1 + w2
    eid_ref[...] = jnp.concatenate([gi * EXPERTS_PER_GROUP + i1, gi * EXPERTS_PER_GROUP + i2], axis=0)
    wrows = jnp.concatenate([w1 / den, w2 / den, jnp.zeros((V7X_LANES - 2, tm), F32)], axis=0)
    wcol_ref[...] = wrows.T


def _route(logits):
    nt = logits.shape[0]
    tm = ROW_TILE
    return pl.pallas_call(
        _route_kernel,
        grid=(nt // tm,),
        in_specs=[pl.BlockSpec((tm, V7X_LANES), lambda i: (i, 0))],
        out_specs=[
            pl.BlockSpec((2, tm), lambda i: (0, i)),
            pl.BlockSpec((tm, V7X_LANES), lambda i: (i, 0)),
        ],
        out_shape=[
            jax.ShapeDtypeStruct((2, nt), jnp.int32),
            jax.ShapeDtypeStruct((nt, V7X_LANES), F32),
        ],
        compiler_params=_params(32),
        name="route",
    )(logits)


def _dispatch_plan(eid, *, n_rows_padded, tile):
    n_pairs = eid.size
    nt = eid.shape[1]
    e_flat = eid.reshape(n_pairs)
    onehot = (e_flat[:, None] == jnp.arange(N_EXPERTS, dtype=jnp.int32)[None, :]).astype(jnp.int32)
    csum = jnp.cumsum(onehot, axis=0)
    counts = csum[-1]
    rank = jnp.sum(onehot * csum, axis=1) - 1
    padded = ((counts + tile - 1) // tile) * tile
    ends = jnp.cumsum(padded)
    starts = ends - padded
    dest = starts[e_flat] + rank
    tok = jnp.arange(n_pairs, dtype=jnp.int32) % nt
    src = jnp.zeros((n_rows_padded,), jnp.int32).at[dest].set(tok, unique_indices=True)
    n_tiles = n_rows_padded // tile
    tile_start = jnp.arange(n_tiles, dtype=jnp.int32) * tile
    tile_expert = jnp.minimum(
        jnp.sum((tile_start[:, None] >= ends[None, :]).astype(jnp.int32), axis=1), N_EXPERTS - 1)
    n_valid = (ends[-1] // tile).astype(jnp.int32).reshape(1)
    last_expert = tile_expert[jnp.maximum(n_valid[0] - 1, 0)]
    tile_expert = jnp.where(jnp.arange(n_tiles) < n_valid[0], tile_expert, last_expert).astype(jnp.int32)
    return src, dest.reshape(2, nt).astype(jnp.int32), tile_expert, n_valid


def _row_copy(src_hbm, row, dst_vmem, r, sem, nblk):
    return pltpu.make_async_copy(
        src_hbm.at[pl.ds(pl.multiple_of(row * nblk, nblk), nblk), :],
        dst_vmem.at[pl.ds(pl.multiple_of(r * nblk, nblk), nblk), :], sem)


def _expert_up_kernel(te_ref, nvalid_ref, src_ref, src_next_ref, x_hbm, wg_ref, wu_ref, h_ref,
                      xbuf, sem):
    del te_ref
    t = pl.program_id(0)
    nv = nvalid_ref[0]
    tm = h_ref.shape[0]
    nblk = xbuf.shape[1] // tm
    slot = lax.rem(t, 2)

    def issue(idx_ref, s):
        def body(g, carry):
            for j in range(DMA_UNROLL):
                r = g * DMA_UNROLL + j
                _row_copy(x_hbm, idx_ref[0, 0, r], xbuf.at[s], r, sem.at[s], nblk).start(
                    priority=j % 2)
            return carry
        lax.fori_loop(0, tm // DMA_UNROLL, body, 0)

    def drain(s):
        def body(g, carry):
            for j in range(DMA_UNROLL):
                _row_copy(x_hbm, 0, xbuf.at[s], g * DMA_UNROLL + j, sem.at[s], nblk).wait()
            return carry
        lax.fori_loop(0, tm // DMA_UNROLL, body, 0)

    @pl.when((t == 0) & (nv > 0))
    def _():
        issue(src_ref, 0)

    @pl.when(t + 1 < nv)
    def _():
        issue(src_next_ref, 1 - slot)

    @pl.when(t < nv)
    def _():
        drain(slot)
        x = _load_token_major(xbuf.at[slot], 0, tm, nblk).astype(BF16).astype(F32)
        g = jnp.dot(x, wg_ref[...], preferred_element_type=F32)
        u = jnp.dot(x, wu_ref[...], preferred_element_type=F32)
        h_ref[...] = (_silu(g) * u).astype(h_ref.dtype)

    @pl.when(t >= nv)
    def _():
        h_ref[...] = jnp.zeros_like(h_ref)


def _expert_down_kernel(te_ref, nvalid_ref, h_ref, wd_ref, o_ref):
    del te_ref
    t = pl.program_id(0)

    @pl.when(t < nvalid_ref[0])
    def _():
        _store_token_major(
            o_ref, jnp.dot(h_ref[...].astype(F32), wd_ref[...], preferred_element_type=F32))

    @pl.when(t >= nvalid_ref[0])
    def _():
        o_ref[...] = jnp.zeros_like(o_ref)


def _experts(x1t, src, w_gate, w_up, w_down, tile_expert, n_valid, layer, *, tile):
    d, dff = w_gate.shape[-2:]
    nblk = d // V7X_LANES
    r = src.shape[0]
    n_tiles = r // tile
    src3 = src.reshape(n_tiles, 1, tile)
    wspec = lambda a, b: pl.BlockSpec((None, None, a, b), lambda t, te, nv: (layer, te[t], 0, 0))
    h = pl.pallas_call(
        _expert_up_kernel,
        grid_spec=pltpu.PrefetchScalarGridSpec(
            num_scalar_prefetch=2,
            grid=(n_tiles,),
            in_specs=[
                pl.BlockSpec((1, 1, tile), lambda t, te, nv: (t, 0, 0), memory_space=pltpu.SMEM),
                pl.BlockSpec((1, 1, tile), lambda t, te, nv: (jnp.minimum(t + 1, n_tiles - 1), 0, 0),
                             memory_space=pltpu.SMEM),
                pl.BlockSpec(memory_space=pl.ANY),
                wspec(d, dff), wspec(d, dff),
            ],
            out_specs=pl.BlockSpec((tile, dff), lambda t, te, nv: (t, 0)),
            scratch_shapes=[pltpu.VMEM((2, tile * nblk, V7X_LANES), F32),
                            pltpu.SemaphoreType.DMA((2,))],
        ),
        out_shape=jax.ShapeDtypeStruct((r, dff), BF16),
        compiler_params=_params(56),
        name="moe_up",
    )(tile_expert, n_valid, src3, src3, x1t, w_gate, w_up)
    return pl.pallas_call(
        _expert_down_kernel,
        grid_spec=pltpu.PrefetchScalarGridSpec(
            num_scalar_prefetch=2,
            grid=(n_tiles,),
            in_specs=[
                pl.BlockSpec((tile, dff), lambda t, te, nv: (t, 0)),
                wspec(dff, d),
            ],
            out_specs=pl.BlockSpec((tile * nblk, V7X_LANES), lambda t, te, nv: (t, 0)),
        ),
        out_shape=jax.ShapeDtypeStruct((r * nblk, V7X_LANES), F32),
        compiler_params=_params(40),
        name="moe_down",
    )(tile_expert, n_valid, h, w_down)


def _combine_kernel(pos_ref, x_ref, wcol_ref, g_ref, b_ref, ys_hbm, x2_ref, x2b_ref,
                    buf0, buf1, sem, *, alpha):
    tm = x_ref.shape[0]
    part = tm // COMBINE_PARTS
    nblk = buf0.shape[0] // tm

    def copies(p, r, first_row, second_row):
        return (_row_copy(ys_hbm, first_row, buf0, r, sem.at[p], nblk),
                _row_copy(ys_hbm, second_row, buf1, r, sem.at[p], nblk))

    def issue(p):
        def body(g, carry):
            for j in range(DMA_UNROLL):
                r = p * part + g * DMA_UNROLL + j
                for cp in copies(p, r, pos_ref[0, 0, r], pos_ref[0, 1, r]):
                    cp.start(priority=j % 2)
            return carry
        lax.fori_loop(0, part // DMA_UNROLL, body, 0)

    def drain(p):
        def body(g, carry):
            for j in range(DMA_UNROLL):
                for cp in copies(p, p * part + g * DMA_UNROLL + j, 0, 0):
                    cp.wait()
            return carry
        lax.fori_loop(0, part // DMA_UNROLL, body, 0)

    for p in range(COMBINE_PARTS):
        issue(p)
    for p in range(COMBINE_PARTS):
        drain(p)
        rows = slice(p * part, (p + 1) * part)
        wc = wcol_ref[rows, :]
        y0 = _load_token_major(buf0, p * part, part, nblk)
        y1 = _load_token_major(buf1, p * part, part, nblk)
        z = alpha * x_ref[rows, :] + wc[:, 0:1] * y0 + wc[:, 1:2] * y1
        x2 = _layer_norm(z, g_ref[...], b_ref[...])
        x2_ref[rows, :] = x2
        x2b_ref[rows, :] = x2.astype(BF16)


def _combine(pos, x1, wcol, ln_g, ln_b, ys, *, alpha):
    nt, d = x1.shape
    tm = COMBINE_ROW_TILE
    n_steps = nt // tm
    pos_t = pos.reshape(2, n_steps, tm).transpose(1, 0, 2)
    return pl.pallas_call(
        functools.partial(_combine_kernel, alpha=alpha),
        grid=(n_steps,),
        in_specs=[
            pl.BlockSpec((1, 2, tm), lambda i: (i, 0, 0), memory_space=pltpu.SMEM),
            pl.BlockSpec((tm, d), lambda i: (i, 0)),
            pl.BlockSpec((tm, V7X_LANES), lambda i: (i, 0)),
            pl.BlockSpec((1, d), lambda i: (0, 0)),
            pl.BlockSpec((1, d), lambda i: (0, 0)),
            pl.BlockSpec(memory_space=pl.ANY),
        ],
        out_specs=[pl.BlockSpec((tm, d), lambda i: (i, 0)), pl.BlockSpec((tm, d), lambda i: (i, 0))],
        out_shape=[jax.ShapeDtypeStruct((nt, d), F32), jax.ShapeDtypeStruct((nt, d), BF16)],
        scratch_shapes=[pltpu.VMEM((tm * (d // V7X_LANES), V7X_LANES), F32)] * 2
                       + [pltpu.SemaphoreType.DMA((COMBINE_PARTS,))],
        compiler_params=_params(56),
        name="moe_combine_ln2",
    )(pos_t, x1, wcol, ln_g, ln_b, ys)


def kernel(x_prompt, x_sample, state_hgrn, w_in, sgu_ln_g, sgu_ln_b, w_s, b_s, lb_logits,
           hgrn_norm_g, w_out_a, w_out_b, w_o, ln1_g, ln1_b, router_w, router_b, w_gate, w_up,
           w_down, ln2_g, ln2_b):
    batch, seq, d = x_prompt.shape
    n_seq = x_sample.shape[0]
    depth = w_in.shape[0]
    a_width = sgu_ln_g.shape[1]
    n_prompt = batch * seq
    nt = n_prompt + n_seq
    alpha = (2.0 * depth) ** 0.25
    assert x_sample.shape[1] == 1 and n_seq == SGU_CHUNK and nt % ROW_TILE == 0
    assert a_width == SGU_GROUPS * SGU_CHUNK and d == 2 * a_width

    p = jax.nn.softmax(lb_logits.astype(F32), axis=0)
    lower_bounds = jnp.cumsum(p, axis=0) - p[0:1]

    n_rows_padded = 2 * nt + N_EXPERTS * MOE_ROW_TILE
    n_rows_padded = -(-n_rows_padded // MOE_ROW_TILE) * MOE_ROW_TILE

    woa_b, wob_b, wo_b = (a.astype(BF16) for a in (w_out_a, w_out_b, w_o))
    rw_pad = jnp.zeros((d, V7X_LANES), F32).at[:, :N_EXPERTS].set(router_w.astype(F32))
    rw_hi = rw_pad.astype(BF16)
    rw_mid = (rw_pad - rw_hi.astype(F32)).astype(BF16)
    rw3 = jnp.concatenate([rw_hi, rw_hi, rw_mid], axis=0)
    rb_pad = jnp.zeros((1, V7X_LANES), F32).at[0, :N_EXPERTS].set(router_b.astype(F32))
    s_sample = jnp.zeros(state_hgrn.shape, F32)

    causal = jnp.tril(jnp.ones((SGU_CHUNK, SGU_CHUNK), bool))
    eye = jnp.eye(SGU_CHUNK, dtype=F32)
    gd = a_width // SGU_GROUPS

    x = jnp.concatenate([x_prompt.reshape(n_prompt, d), x_sample.reshape(n_seq, d)], axis=0)
    x_b = x.astype(BF16)
    s_prompt, v_sample = [], []
    for l in range(depth):
        w_eff = jnp.stack([
            jnp.where(causal[None], w_s[l], 0.0),
            w_s[l][:, 0:1, 0:1] * eye[None],
        ]).astype(BF16)
        bias_eff = jnp.stack([
            jnp.repeat(b_s[l].T, gd, axis=1),
            jnp.broadcast_to(jnp.repeat(b_s[l][:, 0], gd)[None, :], (SGU_CHUNK, a_width)),
        ]).astype(F32)
        lb_l = lower_bounds[l].reshape(1, -1)
        ng_l = hgrn_norm_g[l].astype(F32).reshape(1, -1)

        proj = _matmul(x_b, w_in, l, tm=PROJ_ROW_TILE, tn=PROJ_COL_TILE, out_dtype=BF16)
        y_a, v_rows = _sgu(proj, w_eff, bias_eff, sgu_ln_g[l].reshape(1, -1).astype(F32),
                           sgu_ln_b[l].reshape(1, -1).astype(F32),
                           n_prompt_rows=n_prompt, width=a_width)
        q_col0 = 2 * a_width // HGRN_DK
        y_b, s_p = _hgrn_prompt(proj, lb_l, ng_l, batch=batch, seq=seq, n_rows_out=nt, col0=q_col0)
        y_b, s_sample = _hgrn_sample(proj, lb_l, ng_l, state_hgrn, y_b, s_sample, l,
                                     n_prompt_rows=n_prompt, n_seq=n_seq, col0=q_col0)
        gate_col_block = (2 * a_width + 4 * HGRN_HEADS * HGRN_DK) // d
        x1, x1t, logits = _merge(y_a, y_b, proj, x, woa_b, wob_b, wo_b,
                            ln1_g[l].reshape(1, -1).astype(F32), ln1_b[l].reshape(1, -1).astype(F32),
                            rw3, rb_pad, l, alpha=alpha, gate_col_block=gate_col_block)
        eid, wcol = _route(logits)
        src, pos, tile_expert, n_valid = _dispatch_plan(
            eid, n_rows_padded=n_rows_padded, tile=MOE_ROW_TILE)
        ys = _experts(x1t, src, w_gate, w_up, w_down, tile_expert, n_valid, l, tile=MOE_ROW_TILE)
        x, x_b = _combine(pos, x1, wcol, ln2_g[l].reshape(1, -1).astype(F32),
                          ln2_b[l].reshape(1, -1).astype(F32), ys, alpha=alpha)
        s_prompt.append(s_p)
        v_sample.append(v_rows.reshape(n_seq, 1, a_width))

    y_prompt = x[:n_prompt].reshape(batch, seq, d)
    y_sample = x[n_prompt:].reshape(n_seq, 1, d)
    return (y_prompt, y_sample, jnp.stack(s_prompt), s_sample, jnp.stack(v_sample))
```

```python
import functools

import jax
import jax.numpy as jnp
from jax import lax
from jax.experimental import pallas as pl
from jax.experimental.pallas import tpu as pltpu

F32 = jnp.float32
BF16 = jnp.bfloat16

SGU_GROUPS = 8
SGU_CHUNK = 128
HGRN_HEADS = 8
HGRN_DK = 128
HGRN_DV = 128
N_EXPERTS = 16
N_EXPERT_GROUPS = 4
EXPERTS_PER_GROUP = N_EXPERTS // N_EXPERT_GROUPS
LN_EPS = 1e-5

V7X_LANES = 128
V7X_SUBLANES = 8
V7X_VMEM_BYTES = 64 * 1024 * 1024

ROW_TILE = 640
PROJ_ROW_TILE = 832
MERGE_ROW_TILE = 416
COMBINE_PARTS = 2
DMA_UNROLL = 8
PROJ_COL_TILE = 1024
HGRN_CHUNK = 128
HGRN_BAND = 8
HGRN_PAD = 64
HGRN_HEADS_PER_STEP = 4
SAMPLE_SEQ_TILE = 16
MOE_ROW_TILE = 256
COMBINE_ROW_TILE = 640


def _params(vmem_mb, n_axes=1):
    return pltpu.CompilerParams(
        dimension_semantics=("arbitrary",) * n_axes,
        vmem_limit_bytes=vmem_mb * 1024 * 1024,
    )


def _layer_norm(x, g, b):
    mu = jnp.mean(x, axis=-1, keepdims=True)
    xc = x - mu
    var = jnp.mean(xc * xc, axis=-1, keepdims=True)
    return xc * lax.rsqrt(var + LN_EPS) * g + b


def _silu(x):
    return x * jax.nn.sigmoid(x)


def _matmul_kernel(x_ref, w_ref, o_ref, wb_ref):
    @pl.when(pl.program_id(1) == 0)
    def _():
        wb_ref[...] = w_ref[...].astype(wb_ref.dtype)

    o_ref[...] = jnp.dot(x_ref[...], wb_ref[...], preferred_element_type=F32).astype(o_ref.dtype)


def _matmul(x, w, layer, *, tm, tn, out_dtype):
    m, k = x.shape
    n = w.shape[-1]
    return pl.pallas_call(
        _matmul_kernel,
        grid=(n // tn, m // tm),
        in_specs=[
            pl.BlockSpec((tm, k), lambda j, i: (i, 0)),
            pl.BlockSpec((None, k, tn), lambda j, i: (layer, 0, j)),
        ],
        out_specs=pl.BlockSpec((tm, tn), lambda j, i: (i, j)),
        out_shape=jax.ShapeDtypeStruct((m, n), out_dtype),
        scratch_shapes=[pltpu.VMEM((k, tn), BF16)],
        compiler_params=_params(48, 2),
        name="proj_matmul",
    )(x, w)


def _sgu_kernel(u_ref, v_ref, w_ref, bias_ref, g_ref, b_ref, ya_ref, vrow_ref, *,
                chunks_per_tile, n_prompt_chunks, n_steps):
    i = pl.program_id(0)
    u = jax.nn.gelu(u_ref[...].astype(F32))
    vn = _layer_norm(jax.nn.gelu(v_ref[...].astype(F32)), g_ref[...], b_ref[...])
    gd = vn.shape[-1] // SGU_GROUPS
    for c in range(chunks_per_tile):
        rows = slice(c * SGU_CHUNK, (c + 1) * SGU_CHUNK)
        variant = (i * chunks_per_tile + c >= n_prompt_chunks).astype(jnp.int32)
        w = w_ref[variant]
        vc = vn[rows].astype(BF16)
        parts = [
            jnp.dot(w[g], vc[:, g * gd:(g + 1) * gd], preferred_element_type=F32)
            for g in range(SGU_GROUPS)
        ]
        s = jnp.concatenate(parts, axis=-1) + bias_ref[variant]
        ya_ref[rows, :] = (u[rows] * s).astype(ya_ref.dtype)

    @pl.when(i == n_steps - 1)
    def _():
        last = slice((chunks_per_tile - 1) * SGU_CHUNK, chunks_per_tile * SGU_CHUNK)
        vrow_ref[...] = vn[last]


def _sgu(proj, w_eff, bias_eff, ln_g, ln_b, *, n_prompt_rows, width):
    nt = proj.shape[0]
    tr = ROW_TILE
    n_steps = nt // tr
    kern = functools.partial(
        _sgu_kernel, chunks_per_tile=tr // SGU_CHUNK,
        n_prompt_chunks=n_prompt_rows // SGU_CHUNK, n_steps=n_steps)
    return pl.pallas_call(
        kern,
        grid=(n_steps,),
        in_specs=[
            pl.BlockSpec((tr, width), lambda i: (i, 0)),
            pl.BlockSpec((tr, width), lambda i: (i, 1)),
            pl.BlockSpec(w_eff.shape, lambda i: (0, 0, 0, 0)),
            pl.BlockSpec(bias_eff.shape, lambda i: (0, 0, 0)),
            pl.BlockSpec((1, width), lambda i: (0, 0)),
            pl.BlockSpec((1, width), lambda i: (0, 0)),
        ],
        out_specs=[
            pl.BlockSpec((tr, width), lambda i: (i, 0)),
            pl.BlockSpec((SGU_CHUNK, width), lambda i: (0, 0)),
        ],
        out_shape=[
            jax.ShapeDtypeStruct((nt, width), BF16),
            jax.ShapeDtypeStruct((SGU_CHUNK, width), F32),
        ],
        compiler_params=_params(40),
        name="sgu",
    )(proj, proj, w_eff, bias_eff, ln_g, ln_b)


def _hgrn_gates(qp, z, lb):
    q = _silu(qp)
    e = jnp.exp(-jnp.abs(z))
    s_big = 1.0 / (1.0 + e)
    s_small = e * s_big
    pos = z >= 0.0
    one_m_lb = 1.0 - lb
    forget = lb + one_m_lb * jnp.where(pos, s_big, s_small)
    k = one_m_lb * jnp.where(pos, s_small, s_big)
    return q, forget, k


def _hgrn_prompt_kernel(q_ref, f_ref, i_ref, g_ref, lb_ref, ng_ref, yin_ref, y_ref, s_ref,
                        st_ref, lvl_ref, kpad, gpad, vpad, *, n_chunks):
    del yin_ref
    c = HGRN_CHUNK
    w = HGRN_BAND
    pad = HGRN_PAD
    nh = HGRN_HEADS_PER_STEP
    dk = HGRN_DK

    row = lax.broadcasted_iota(jnp.int32, (c, c), 0)
    col = lax.broadcasted_iota(jnp.int32, (c, c), 1)
    x = row ^ col
    lvl = jnp.where(x >= 64, 64, jnp.where(x >= 32, 32, jnp.where(x >= 16, 16, jnp.where(x >= 8, 8, 0))))
    lvl_ref[...] = jnp.where(row - col >= w, lvl, 0)
    st_ref[...] = jnp.zeros_like(st_ref)
    for ref in (kpad, gpad, vpad):
        ref[:, 0:pad, :] = jnp.zeros((nh, pad, dk), F32)
    r2 = lax.broadcasted_iota(jnp.int32, (2 * dk, 2 * dk), 0)
    c2 = lax.broadcasted_iota(jnp.int32, (2 * dk, 2 * dk), 1)
    pair_ones = jnp.where((r2 >= dk) == (c2 >= dk), 1.0, 0.0).astype(BF16)
    cur = pl.ds(pad, c)

    def head_chunk(hh, sl):
        cs = slice(hh * dk, (hh + 1) * dk)
        q, forget, k = _hgrn_gates(q_ref[sl, cs].astype(F32), f_ref[sl, cs].astype(F32),
                                   lb_ref[:, cs])
        v = i_ref[sl, cs].astype(F32)
        gcum = jnp.log2(forget)
        shift = 1
        while shift < c:
            gpad[hh, cur, :] = gcum
            gcum = gcum + gpad[hh, pl.ds(pad - shift, c), :]
            shift *= 2
        gpad[hh, cur, :] = gcum
        kpad[hh, cur, :] = k
        vpad[hh, cur, :] = v

        parts = []
        for d in range(w):
            if d == 0:
                p = q * k
            else:
                back = pl.ds(pad - d, c)
                p = q * kpad[hh, back, :] * jnp.exp2(gcum - gpad[hh, back, :])
            parts.append(p.astype(BF16))
        pairs = [jnp.concatenate(parts[2 * j:2 * j + 2], axis=1) for j in range(w // 2)]
        rs = jnp.dot(jnp.concatenate(pairs, axis=0), pair_ones, preferred_element_type=F32)
        o = rs[0:c, 0:dk] * v
        for d in range(1, w):
            j, half = divmod(d, 2)
            o = o + rs[j * c:(j + 1) * c, half * dk:(half + 1) * dk] * vpad[hh, pl.ds(pad - d, c), :]

        lv = lvl_ref[...]
        scores = jnp.zeros((c, c), F32)
        b = w
        while b < c:
            seg = gcum.reshape(c // (2 * b), 2 * b, dk)
            ref = jnp.broadcast_to(seg[:, b - 1:b, :], seg.shape).reshape(c, dk)
            a = jnp.exp2(-jnp.abs(gcum - ref))
            sc = lax.dot_general((q * a).astype(BF16), (k * a).astype(BF16),
                                 (((1,), (1,)), ((), ())), preferred_element_type=F32)
            scores = jnp.where(lv == b, sc, scores)
            b *= 2
        o = o + jnp.dot(scores.astype(BF16), v.astype(BF16), preferred_element_type=F32)

        st = st_ref[hh]
        qs = (q * jnp.exp2(gcum)).astype(BF16)
        o = o + lax.dot_general(qs, st.astype(BF16), (((1,), (1,)), ((), ())),
                                preferred_element_type=F32)
        glast = gcum[c - 1:c, :]
        ks = (k * jnp.exp2(glast - gcum)).astype(BF16)
        st_ref[hh] = st * jnp.exp2(glast) + jnp.dot(v.T.astype(BF16), ks, preferred_element_type=F32)

        ms = jnp.mean(o * o, axis=-1, keepdims=True)
        gate = ng_ref[:, cs] * _silu(g_ref[sl, cs].astype(F32))
        y_ref[sl, cs] = (o * lax.rsqrt(ms + LN_EPS) * gate).astype(y_ref.dtype)

    def chunk(ci, carry):
        sl = pl.ds(pl.multiple_of(ci * c, c), c)
        for hh in range(nh):
            head_chunk(hh, sl)
        return carry

    lax.fori_loop(0, n_chunks, chunk, 0)
    for hh in range(nh):
        s_ref[0, hh] = st_ref[hh].T


def _hgrn_prompt(proj, lb, ng, *, batch, seq, n_rows_out, col0):
    nh = HGRN_HEADS
    hs = HGRN_HEADS_PER_STEP
    wb = hs * HGRN_DK
    cb0 = col0 * HGRN_DK // wb
    nhb = nh // hs
    blk = lambda off: pl.BlockSpec((seq, wb), lambda b, h: (b, cb0 + off * nhb + h))
    vec = pl.BlockSpec((1, wb), lambda b, h: (0, h))
    kern = functools.partial(_hgrn_prompt_kernel, n_chunks=seq // HGRN_CHUNK)
    padded = pltpu.VMEM((hs, HGRN_PAD + HGRN_CHUNK, HGRN_DK), F32)
    return pl.pallas_call(
        kern,
        grid=(batch, nhb),
        in_specs=[blk(0), blk(1), blk(2), blk(3), vec, vec, pl.BlockSpec(memory_space=pl.ANY)],
        out_specs=[
            pl.BlockSpec((seq, wb), lambda b, h: (b, h)),
            pl.BlockSpec((1, hs, HGRN_DK, HGRN_DV), lambda b, h: (b, h, 0, 0)),
        ],
        out_shape=[
            jax.ShapeDtypeStruct((n_rows_out, nh * HGRN_DV), BF16),
            jax.ShapeDtypeStruct((batch, nh, HGRN_DK, HGRN_DV), F32),
        ],
        scratch_shapes=[
            pltpu.VMEM((hs, HGRN_DV, HGRN_DK), F32),
            pltpu.VMEM((HGRN_CHUNK, HGRN_CHUNK), jnp.int32),
            padded, padded, padded,
        ],
        input_output_aliases={6: 0},
        compiler_params=_params(32, 2),
        name="hgrn_prompt",
    )(proj, proj, proj, proj, lb, ng, jnp.zeros((n_rows_out, nh * HGRN_DV), BF16))


def _hgrn_sample_kernel(q_ref, f_ref, i_ref, g_ref, lb_ref, ng_ref, s_ref, yin_ref, sall_ref,
                        y_ref, so_ref, q_scr, f_scr, k_scr, v_scr, o_scr):
    del yin_ref, sall_ref
    nb = q_ref.shape[0]
    q, forget, k = _hgrn_gates(q_ref[...].astype(F32), f_ref[...].astype(F32), lb_ref[...])
    q_scr[...] = q
    f_scr[...] = forget
    k_scr[...] = k
    v_scr[...] = i_ref[...].astype(F32)
    sq = (HGRN_DK, HGRN_DV)

    def per_seq(n, carry):
        r = pl.ds(n, 1)
        qn, fn, kn, vn = q_scr[r, :], f_scr[r, :], k_scr[r, :], v_scr[r, :]
        for h in range(HGRN_HEADS):
            cs = slice(h * HGRN_DK, (h + 1) * HGRN_DK)
            col = lambda a: jnp.broadcast_to(a[:, cs], sq).T
            s_new = col(fn) * s_ref[n, h] + col(kn) * vn[:, h * HGRN_DV:(h + 1) * HGRN_DV]
            so_ref[n, h] = s_new
            o_scr[n, :, h * HGRN_DV:(h + 1) * HGRN_DV] = jnp.dot(
                qn[:, cs].astype(BF16), s_new.astype(BF16), preferred_element_type=F32)
        return carry

    lax.fori_loop(0, nb, per_seq, 0)
    o = jnp.concatenate([o_scr[n] for n in range(nb)], axis=0)
    gate = ng_ref[...] * _silu(g_ref[...].astype(F32))
    outs = []
    for h in range(HGRN_HEADS):
        oh = o[:, h * HGRN_DV:(h + 1) * HGRN_DV]
        ms = jnp.mean(oh * oh, axis=-1, keepdims=True)
        outs.append(oh * lax.rsqrt(ms + LN_EPS))
    y_ref[...] = (jnp.concatenate(outs, axis=-1) * gate).astype(y_ref.dtype)


def _hgrn_sample(proj, lb, ng, state, y_b, state_out, layer, *, n_prompt_rows, n_seq, col0):
    nb = SAMPLE_SEQ_TILE
    r0 = n_prompt_rows // nb
    wk = HGRN_HEADS * HGRN_DK
    cb0 = col0 * HGRN_DK // wk
    blk = lambda off: pl.BlockSpec((nb, wk), lambda i: (r0 + i, cb0 + off))
    vec = pl.BlockSpec((1, wk), lambda i: (0, 0))
    st_spec = pl.BlockSpec((None, nb, HGRN_HEADS, HGRN_DK, HGRN_DV), lambda i: (layer, i, 0, 0, 0))
    anywhere = pl.BlockSpec(memory_space=pl.ANY)
    return pl.pallas_call(
        _hgrn_sample_kernel,
        grid=(n_seq // nb,),
        in_specs=[blk(0), blk(1), blk(2), blk(3), vec, vec, st_spec, anywhere, anywhere],
        out_specs=[pl.BlockSpec((nb, wk), lambda i: (r0 + i, 0)), st_spec],
        out_shape=[
            jax.ShapeDtypeStruct(y_b.shape, y_b.dtype),
            jax.ShapeDtypeStruct(state_out.shape, state_out.dtype),
        ],
        scratch_shapes=[pltpu.VMEM((nb, wk), F32)] * 4 + [pltpu.VMEM((nb, 1, wk), F32)],
        input_output_aliases={7: 0, 8: 1},
        compiler_params=_params(48),
        name="hgrn_sample",
    )(proj, proj, proj, proj, lb, ng, state, y_b, state_out)


def _merge_kernel(ya_ref, yb_ref, ga_ref, gb_ref, x_ref, woa_ref, wob_ref, wo_ref,
                  g_ref, b_ref, rw_ref, rb_ref, x1_ref, logit_ref, z_scr, *, alpha):
    @pl.when(pl.program_id(0) == 0)
    def _():
        z_scr[...] = jnp.zeros_like(z_scr)

    x1 = _layer_norm(z_scr[...], g_ref[...], b_ref[...])
    x1_ref[...] = x1
    hi = x1.astype(BF16)
    mid = (x1 - hi.astype(F32)).astype(BF16)
    logit_ref[...] = jnp.dot(jnp.concatenate([hi, mid, hi], axis=1), rw_ref[...],
                             preferred_element_type=F32) + rb_ref[...]

    a = jnp.dot(ya_ref[...], woa_ref[...], preferred_element_type=F32)
    b = jnp.dot(yb_ref[...], wob_ref[...], preferred_element_type=F32)
    merged = (jax.nn.sigmoid(ga_ref[...].astype(F32)) * a
              + jax.nn.sigmoid(gb_ref[...].astype(F32)) * b)
    z_scr[...] = alpha * x_ref[...] + jnp.dot(merged.astype(BF16), wo_ref[...],
                                              preferred_element_type=F32)


def _merge(y_a, y_b, proj, x, woa, wob, wo, ln_g, ln_b, rw, rb, layer, *, alpha, gate_col_block):
    nt, d = x.shape
    wa = y_a.shape[1]
    tm = MERGE_ROW_TILE
    n_tiles = nt // tm
    const = lambda shape: pl.BlockSpec(shape, lambda i: (0,) * len(shape),
                                       pipeline_mode=pl.Buffered(1))
    stacked = lambda a: pl.BlockSpec((None,) + a.shape[1:], lambda i: (layer, 0, 0),
                                     pipeline_mode=pl.Buffered(1))
    cur = lambda col: (lambda i: (jnp.minimum(i, n_tiles - 1), col))
    prev = lambda i: (jnp.maximum(i - 1, 0), 0)
    return pl.pallas_call(
        functools.partial(_merge_kernel, alpha=alpha),
        grid=(n_tiles + 1,),
        in_specs=[
            pl.BlockSpec((tm, wa), cur(0)),
            pl.BlockSpec((tm, wa), cur(0)),
            pl.BlockSpec((tm, d), cur(gate_col_block)),
            pl.BlockSpec((tm, d), cur(gate_col_block + 1)),
            pl.BlockSpec((tm, d), cur(0)),
            stacked(woa), stacked(wob), stacked(wo),
            const(ln_g.shape), const(ln_b.shape), const(rw.shape), const(rb.shape),
        ],
        out_specs=[
            pl.BlockSpec((tm, d), prev),
            pl.BlockSpec((tm, V7X_LANES), prev),
        ],
        out_shape=[
            jax.ShapeDtypeStruct((nt, d), F32),
            jax.ShapeDtypeStruct((nt, V7X_LANES), F32),
        ],
        scratch_shapes=[pltpu.VMEM((tm, d), F32)],
        compiler_params=_params(56),
        name="merge_ln1",
    )(y_a, y_b, proj, proj, x, woa, wob, wo, ln_g, ln_b, rw, rb)


def _route_kernel(logit_ref, eid_ref, wcol_ref):
    tm = logit_ref.shape[0]
    lt = logit_ref[...].T[0:N_EXPERTS, :]
    e = jnp.exp(lt - jnp.max(lt, axis=0, keepdims=True))
    p = e / jnp.sum(e, axis=0, keepdims=True)
    pr = [p[j:j + 1, :] for j in range(N_EXPERTS)]

    best, gi = None, None
    for g in range(N_EXPERT_GROUPS):
        a, b, c, d = pr[4 * g:4 * g + 4]
        hi1, lo1 = jnp.maximum(a, b), jnp.minimum(a, b)
        hi2, lo2 = jnp.maximum(c, d), jnp.minimum(c, d)
        gs = jnp.maximum(hi1, hi2) + jnp.maximum(jnp.minimum(hi1, hi2), jnp.maximum(lo1, lo2))
        if g == 0:
            best, gi = gs, jnp.zeros((1, tm), jnp.int32)
        else:
            upd = gs > best
            best = jnp.where(upd, gs, best)
            gi = jnp.where(upd, g, gi)

    cand = []
    for j in range(EXPERTS_PER_GROUP):
        cj = pr[j]
        for g in range(1, N_EXPERT_GROUPS):
            cj = jnp.where(gi == g, pr[4 * g + j], cj)
        cand.append(cj)
    w1, i1 = cand[0], jnp.zeros((1, tm), jnp.int32)
    for j in range(1, EXPERTS_PER_GROUP):
        upd = cand[j] > w1
        w1 = jnp.where(upd, cand[j], w1)
        i1 = jnp.where(upd, j, i1)
    w2, i2 = jnp.full((1, tm), -1.0, F32), jnp.zeros((1, tm), jnp.int32)
    for j in range(EXPERTS_PER_GROUP):
        upd = (i1 != j) & (cand[j] > w2)
        w2 = jnp.where(upd, cand[j], w2)
        i2 = jnp.where(upd, j, i2)
    den = w1 + w2
    eid_ref[...] = jnp.concatenate([gi * EXPERTS_PER_GROUP + i1, gi * EXPERTS_PER_GROUP + i2], axis=0)
    wrows = jnp.concatenate([w1 / den, w2 / den, jnp.zeros((V7X_LANES - 2, tm), F32)], axis=0)
    wcol_ref[...] = wrows.T


def _route(logits):
    nt = logits.shape[0]
    tm = ROW_TILE
    return pl.pallas_call(
        _route_kernel,
        grid=(nt // tm,),
        in_specs=[pl.BlockSpec((tm, V7X_LANES), lambda i: (i, 0))],
        out_specs=[
            pl.BlockSpec((2, tm), lambda i: (0, i)),
            pl.BlockSpec((tm, V7X_LANES), lambda i: (i, 0)),
        ],
        out_shape=[
            jax.ShapeDtypeStruct((2, nt), jnp.int32),
            jax.ShapeDtypeStruct((nt, V7X_LANES), F32),
        ],
        compiler_params=_params(32),
        name="route",
    )(logits)


def _dispatch_plan(eid, *, n_rows_padded, tile):
    n_pairs = eid.size
    nt = eid.shape[1]
    e_flat = eid.reshape(n_pairs)
    onehot = (e_flat[:, None] == jnp.arange(N_EXPERTS, dtype=jnp.int32)[None, :]).astype(jnp.int32)
    csum = jnp.cumsum(onehot, axis=0)
    counts = csum[-1]
    rank = jnp.sum(onehot * csum, axis=1) - 1
    padded = ((counts + tile - 1) // tile) * tile
    ends = jnp.cumsum(padded)
    starts = ends - padded
    dest = starts[e_flat] + rank
    tok = jnp.arange(n_pairs, dtype=jnp.int32) % nt
    src = jnp.zeros((n_rows_padded,), jnp.int32).at[dest].set(tok, unique_indices=True)
    n_tiles = n_rows_padded // tile
    tile_start = jnp.arange(n_tiles, dtype=jnp.int32) * tile
    tile_expert = jnp.minimum(
        jnp.sum((tile_start[:, None] >= ends[None, :]).astype(jnp.int32), axis=1), N_EXPERTS - 1)
    n_valid = (ends[-1] // tile).astype(jnp.int32).reshape(1)
    last_expert = tile_expert[jnp.maximum(n_valid[0] - 1, 0)]
    tile_expert = jnp.where(jnp.arange(n_tiles) < n_valid[0], tile_expert, last_expert).astype(jnp.int32)
    return src, dest.reshape(2, nt).astype(jnp.int32), tile_expert, n_valid


def _row_copy(src_hbm, row, dst_vmem, r, sem):
    return pltpu.make_async_copy(src_hbm.at[pl.ds(row, 1), :], dst_vmem.at[pl.ds(r, 1), :], sem)


def _expert_up_kernel(te_ref, nvalid_ref, src_ref, src_next_ref, x_hbm, wg_ref, wu_ref, h_ref,
                      xbuf, sem):
    del te_ref
    t = pl.program_id(0)
    nv = nvalid_ref[0]
    tm = xbuf.shape[1]
    slot = lax.rem(t, 2)

    def issue(idx_ref, s):
        def body(g, carry):
            for j in range(DMA_UNROLL):
                r = g * DMA_UNROLL + j
                _row_copy(x_hbm, idx_ref[0, 0, r], xbuf.at[s], r, sem.at[s]).start(priority=j % 2)
            return carry
        lax.fori_loop(0, tm // DMA_UNROLL, body, 0)

    def drain(s):
        def body(g, carry):
            for j in range(DMA_UNROLL):
                _row_copy(x_hbm, 0, xbuf.at[s], g * DMA_UNROLL + j, sem.at[s]).wait()
            return carry
        lax.fori_loop(0, tm // DMA_UNROLL, body, 0)

    @pl.when((t == 0) & (nv > 0))
    def _():
        issue(src_ref, 0)

    @pl.when(t + 1 < nv)
    def _():
        issue(src_next_ref, 1 - slot)

    @pl.when(t < nv)
    def _():
        drain(slot)
        x = xbuf[slot].astype(BF16).astype(F32)
        g = jnp.dot(x, wg_ref[...], preferred_element_type=F32)
        u = jnp.dot(x, wu_ref[...], preferred_element_type=F32)
        h_ref[...] = (_silu(g) * u).astype(h_ref.dtype)

    @pl.when(t >= nv)
    def _():
        h_ref[...] = jnp.zeros_like(h_ref)


def _expert_down_kernel(te_ref, nvalid_ref, h_ref, wd_ref, o_ref):
    del te_ref
    t = pl.program_id(0)

    @pl.when(t < nvalid_ref[0])
    def _():
        o_ref[...] = jnp.dot(h_ref[...].astype(F32), wd_ref[...], preferred_element_type=F32)

    @pl.when(t >= nvalid_ref[0])
    def _():
        o_ref[...] = jnp.zeros_like(o_ref)


def _experts(x1, src, w_gate, w_up, w_down, tile_expert, n_valid, layer, *, tile):
    d, dff = w_gate.shape[-2:]
    r = src.shape[0]
    n_tiles = r // tile
    src3 = src.reshape(n_tiles, 1, tile)
    wspec = lambda a, b: pl.BlockSpec((None, None, a, b), lambda t, te, nv: (layer, te[t], 0, 0))
    h = pl.pallas_call(
        _expert_up_kernel,
        grid_spec=pltpu.PrefetchScalarGridSpec(
            num_scalar_prefetch=2,
            grid=(n_tiles,),
            in_specs=[
                pl.BlockSpec((1, 1, tile), lambda t, te, nv: (t, 0, 0), memory_space=pltpu.SMEM),
                pl.BlockSpec((1, 1, tile), lambda t, te, nv: (jnp.minimum(t + 1, n_tiles - 1), 0, 0),
                             memory_space=pltpu.SMEM),
                pl.BlockSpec(memory_space=pl.ANY),
                wspec(d, dff), wspec(d, dff),
            ],
            out_specs=pl.BlockSpec((tile, dff), lambda t, te, nv: (t, 0)),
            scratch_shapes=[pltpu.VMEM((2, tile, d), F32), pltpu.SemaphoreType.DMA((2,))],
        ),
        out_shape=jax.ShapeDtypeStruct((r, dff), BF16),
        compiler_params=_params(56),
        name="moe_up",
    )(tile_expert, n_valid, src3, src3, x1, w_gate, w_up)
    return pl.pallas_call(
        _expert_down_kernel,
        grid_spec=pltpu.PrefetchScalarGridSpec(
            num_scalar_prefetch=2,
            grid=(n_tiles,),
            in_specs=[
                pl.BlockSpec((tile, dff), lambda t, te, nv: (t, 0)),
                wspec(dff, d),
            ],
            out_specs=pl.BlockSpec((tile, d), lambda t, te, nv: (t, 0)),
        ),
        out_shape=jax.ShapeDtypeStruct((r, d), F32),
        compiler_params=_params(40),
        name="moe_down",
    )(tile_expert, n_valid, h, w_down)


def _combine_kernel(pos_ref, x_ref, wcol_ref, g_ref, b_ref, ys_hbm, x2_ref, x2b_ref,
                    buf0, buf1, sem, *, alpha):
    tm = x_ref.shape[0]
    part = tm // COMBINE_PARTS

    def copies(p, r, first_row, second_row):
        return (_row_copy(ys_hbm, first_row, buf0, r, sem.at[p]),
                _row_copy(ys_hbm, second_row, buf1, r, sem.at[p]))

    def issue(p):
        def body(g, carry):
            for j in range(DMA_UNROLL):
                r = p * part + g * DMA_UNROLL + j
                for cp in copies(p, r, pos_ref[0, 0, r], pos_ref[0, 1, r]):
                    cp.start(priority=j % 2)
            return carry
        lax.fori_loop(0, part // DMA_UNROLL, body, 0)

    def drain(p):
        def body(g, carry):
            for j in range(DMA_UNROLL):
                for cp in copies(p, p * part + g * DMA_UNROLL + j, 0, 0):
                    cp.wait()
            return carry
        lax.fori_loop(0, part // DMA_UNROLL, body, 0)

    for p in range(COMBINE_PARTS):
        issue(p)
    for p in range(COMBINE_PARTS):
        drain(p)
        rows = slice(p * part, (p + 1) * part)
        wc = wcol_ref[rows, :]
        z = alpha * x_ref[rows, :] + wc[:, 0:1] * buf0[rows, :] + wc[:, 1:2] * buf1[rows, :]
        x2 = _layer_norm(z, g_ref[...], b_ref[...])
        x2_ref[rows, :] = x2
        x2b_ref[rows, :] = x2.astype(BF16)


def _combine(pos, x1, wcol, ln_g, ln_b, ys, *, alpha):
    nt, d = x1.shape
    tm = COMBINE_ROW_TILE
    n_steps = nt // tm
    pos_t = pos.reshape(2, n_steps, tm).transpose(1, 0, 2)
    return pl.pallas_call(
        functools.partial(_combine_kernel, alpha=alpha),
        grid=(n_steps,),
        in_specs=[
            pl.BlockSpec((1, 2, tm), lambda i: (i, 0, 0), memory_space=pltpu.SMEM),
            pl.BlockSpec((tm, d), lambda i: (i, 0)),
            pl.BlockSpec((tm, V7X_LANES), lambda i: (i, 0)),
            pl.BlockSpec((1, d), lambda i: (0, 0)),
            pl.BlockSpec((1, d), lambda i: (0, 0)),
            pl.BlockSpec(memory_space=pl.ANY),
        ],
        out_specs=[pl.BlockSpec((tm, d), lambda i: (i, 0)), pl.BlockSpec((tm, d), lambda i: (i, 0))],
        out_shape=[jax.ShapeDtypeStruct((nt, d), F32), jax.ShapeDtypeStruct((nt, d), BF16)],
        scratch_shapes=[pltpu.VMEM((tm, d), F32), pltpu.VMEM((tm, d), F32),
                        pltpu.SemaphoreType.DMA((COMBINE_PARTS,))],
        compiler_params=_params(56),
        name="moe_combine_ln2",
    )(pos_t, x1, wcol, ln_g, ln_b, ys)


def kernel(x_prompt, x_sample, state_hgrn, w_in, sgu_ln_g, sgu_ln_b, w_s, b_s, lb_logits,
           hgrn_norm_g, w_out_a, w_out_b, w_o, ln1_g, ln1_b, router_w, router_b, w_gate, w_up,
           w_down, ln2_g, ln2_b):
    batch, seq, d = x_prompt.shape
    n_seq = x_sample.shape[0]
    depth = w_in.shape[0]
    a_width = sgu_ln_g.shape[1]
    n_prompt = batch * seq
    nt = n_prompt + n_seq
    alpha = (2.0 * depth) ** 0.25
    assert x_sample.shape[1] == 1 and n_seq == SGU_CHUNK and nt % ROW_TILE == 0
    assert a_width == SGU_GROUPS * SGU_CHUNK and d == 2 * a_width

    p = jax.nn.softmax(lb_logits.astype(F32), axis=0)
    lower_bounds = jnp.cumsum(p, axis=0) - p[0:1]

    n_rows_padded = 2 * nt + N_EXPERTS * MOE_ROW_TILE
    n_rows_padded = -(-n_rows_padded // MOE_ROW_TILE) * MOE_ROW_TILE

    woa_b, wob_b, wo_b = (a.astype(BF16) for a in (w_out_a, w_out_b, w_o))
    rw_pad = jnp.zeros((d, V7X_LANES), F32).at[:, :N_EXPERTS].set(router_w.astype(F32))
    rw_hi = rw_pad.astype(BF16)
    rw_mid = (rw_pad - rw_hi.astype(F32)).astype(BF16)
    rw3 = jnp.concatenate([rw_hi, rw_hi, rw_mid], axis=0)
    rb_pad = jnp.zeros((1, V7X_LANES), F32).at[0, :N_EXPERTS].set(router_b.astype(F32))
    s_sample = jnp.zeros(state_hgrn.shape, F32)

    causal = jnp.tril(jnp.ones((SGU_CHUNK, SGU_CHUNK), bool))
    eye = jnp.eye(SGU_CHUNK, dtype=F32)
    gd = a_width // SGU_GROUPS

    x = jnp.concatenate([x_prompt.reshape(n_prompt, d), x_sample.reshape(n_seq, d)], axis=0)
    x_b = x.astype(BF16)
    s_prompt, v_sample = [], []
    for l in range(depth):
        w_eff = jnp.stack([
            jnp.where(causal[None], w_s[l], 0.0),
            w_s[l][:, 0:1, 0:1] * eye[None],
        ]).astype(BF16)
        bias_eff = jnp.stack([
            jnp.repeat(b_s[l].T, gd, axis=1),
            jnp.broadcast_to(jnp.repeat(b_s[l][:, 0], gd)[None, :], (SGU_CHUNK, a_width)),
        ]).astype(F32)
        lb_l = lower_bounds[l].reshape(1, -1)
        ng_l = hgrn_norm_g[l].astype(F32).reshape(1, -1)

        proj = _matmul(x_b, w_in, l, tm=PROJ_ROW_TILE, tn=PROJ_COL_TILE, out_dtype=BF16)
        y_a, v_rows = _sgu(proj, w_eff, bias_eff, sgu_ln_g[l].reshape(1, -1).astype(F32),
                           sgu_ln_b[l].reshape(1, -1).astype(F32),
                           n_prompt_rows=n_prompt, width=a_width)
        q_col0 = 2 * a_width // HGRN_DK
        y_b, s_p = _hgrn_prompt(proj, lb_l, ng_l, batch=batch, seq=seq, n_rows_out=nt, col0=q_col0)
        y_b, s_sample = _hgrn_sample(proj, lb_l, ng_l, state_hgrn, y_b, s_sample, l,
                                     n_prompt_rows=n_prompt, n_seq=n_seq, col0=q_col0)
        gate_col_block = (2 * a_width + 4 * HGRN_HEADS * HGRN_DK) // d
        x1, logits = _merge(y_a, y_b, proj, x, woa_b, wob_b, wo_b,
                            ln1_g[l].reshape(1, -1).astype(F32), ln1_b[l].reshape(1, -1).astype(F32),
                            rw3, rb_pad, l, alpha=alpha, gate_col_block=gate_col_block)
        eid, wcol = _route(logits)
        src, pos, tile_expert, n_valid = _dispatch_plan(
            eid, n_rows_padded=n_rows_padded, tile=MOE_ROW_TILE)
        ys = _experts(x1, src, w_gate, w_up, w_down, tile_expert, n_valid, l, tile=MOE_ROW_TILE)
        x, x_b = _combine(pos, x1, wcol, ln2_g[l].reshape(1, -1).astype(F32),
                          ln2_b[l].reshape(1, -1).astype(F32), ys, alpha=alpha)
        s_prompt.append(s_p)
        v_sample.append(v_rows.reshape(n_seq, 1, a_width))

    y_prompt = x[:n_prompt].reshape(batch, seq, d)
    y_sample = x[n_prompt:].reshape(n_seq, 1, d)
    return (y_prompt, y_sample, jnp.stack(s_prompt), s_sample, jnp.stack(v_sample))
```

```python
import functools

import jax
import jax.numpy as jnp
from jax import lax
from jax.experimental import pallas as pl
from jax.experimental.pallas import tpu as pltpu

F32 = jnp.float32
BF16 = jnp.bfloat16

SGU_GROUPS = 8
SGU_CHUNK = 128
HGRN_HEADS = 8
HGRN_DK = 128
HGRN_DV = 128
N_EXPERTS = 16
N_EXPERT_GROUPS = 4
EXPERTS_PER_GROUP = N_EXPERTS // N_EXPERT_GROUPS
LN_EPS = 1e-5

V7X_LANES = 128

ROW_TILE = 640
PROJ_ROW_TILE = 832
MERGE_ROW_TILE = 416
COMBINE_PARTS = 2
DMA_UNROLL = 8
PROJ_COL_TILE = 1024
HGRN_CHUNK = 128
HGRN_BAND = 8
HGRN_PAD = 64
HGRN_HEADS_PER_STEP = 4
SAMPLE_SEQ_TILE = 16
MOE_ROW_TILE = 256
COMBINE_ROW_TILE = 640


def _params(vmem_mb, n_axes=1):
    return pltpu.CompilerParams(
        dimension_semantics=("arbitrary",) * n_axes,
        vmem_limit_bytes=vmem_mb * 1024 * 1024,
    )


def _layer_norm(x, g, b):
    mu = jnp.mean(x, axis=-1, keepdims=True)
    xc = x - mu
    var = jnp.mean(xc * xc, axis=-1, keepdims=True)
    return xc * lax.rsqrt(var + LN_EPS) * g + b


def _silu(x):
    return x * jax.nn.sigmoid(x)


def _matmul_kernel(x_ref, w_ref, o_ref, wb_ref):
    @pl.when(pl.program_id(1) == 0)
    def _():
        wb_ref[...] = w_ref[...].astype(wb_ref.dtype)

    o_ref[...] = jnp.dot(x_ref[...], wb_ref[...], preferred_element_type=F32).astype(o_ref.dtype)


def _matmul(x, w, layer, *, tm, tn, out_dtype):
    m, k = x.shape
    n = w.shape[-1]
    return pl.pallas_call(
        _matmul_kernel,
        grid=(n // tn, m // tm),
        in_specs=[
            pl.BlockSpec((tm, k), lambda j, i: (i, 0)),
            pl.BlockSpec((None, k, tn), lambda j, i: (layer, 0, j)),
        ],
        out_specs=pl.BlockSpec((tm, tn), lambda j, i: (i, j)),
        out_shape=jax.ShapeDtypeStruct((m, n), out_dtype),
        scratch_shapes=[pltpu.VMEM((k, tn), BF16)],
        compiler_params=_params(48, 2),
        name="proj_matmul",
    )(x, w)


def _sgu_kernel(u_ref, v_ref, w_ref, bias_ref, g_ref, b_ref, ya_ref, vrow_ref, *,
                chunks_per_tile, n_prompt_chunks, n_steps):
    i = pl.program_id(0)
    u = jax.nn.gelu(u_ref[...].astype(F32))
    vn = _layer_norm(jax.nn.gelu(v_ref[...].astype(F32)), g_ref[...], b_ref[...])
    gd = vn.shape[-1] // SGU_GROUPS
    for c in range(chunks_per_tile):
        rows = slice(c * SGU_CHUNK, (c + 1) * SGU_CHUNK)
        variant = (i * chunks_per_tile + c >= n_prompt_chunks).astype(jnp.int32)
        w = w_ref[variant]
        vc = vn[rows].astype(BF16)
        parts = [
            jnp.dot(w[g], vc[:, g * gd:(g + 1) * gd], preferred_element_type=F32)
            for g in range(SGU_GROUPS)
        ]
        s = jnp.concatenate(parts, axis=-1) + bias_ref[variant]
        ya_ref[rows, :] = (u[rows] * s).astype(ya_ref.dtype)

    @pl.when(i == n_steps - 1)
    def _():
        last = slice((chunks_per_tile - 1) * SGU_CHUNK, chunks_per_tile * SGU_CHUNK)
        vrow_ref[...] = vn[last]


def _sgu(proj, w_eff, bias_eff, ln_g, ln_b, *, n_prompt_rows, width):
    nt = proj.shape[0]
    tr = ROW_TILE
    n_steps = nt // tr
    kern = functools.partial(
        _sgu_kernel, chunks_per_tile=tr // SGU_CHUNK,
        n_prompt_chunks=n_prompt_rows // SGU_CHUNK, n_steps=n_steps)
    return pl.pallas_call(
        kern,
        grid=(n_steps,),
        in_specs=[
            pl.BlockSpec((tr, width), lambda i: (i, 0)),
            pl.BlockSpec((tr, width), lambda i: (i, 1)),
            pl.BlockSpec(w_eff.shape, lambda i: (0, 0, 0, 0)),
            pl.BlockSpec(bias_eff.shape, lambda i: (0, 0, 0)),
            pl.BlockSpec((1, width), lambda i: (0, 0)),
            pl.BlockSpec((1, width), lambda i: (0, 0)),
        ],
        out_specs=[
            pl.BlockSpec((tr, width), lambda i: (i, 0)),
            pl.BlockSpec((SGU_CHUNK, width), lambda i: (0, 0)),
        ],
        out_shape=[
            jax.ShapeDtypeStruct((nt, width), BF16),
            jax.ShapeDtypeStruct((SGU_CHUNK, width), F32),
        ],
        compiler_params=_params(40),
        name="sgu",
    )(proj, proj, w_eff, bias_eff, ln_g, ln_b)


def _hgrn_gates(qp, z, lb):
    q = _silu(qp)
    e = jnp.exp(-jnp.abs(z))
    s_big = 1.0 / (1.0 + e)
    s_small = e * s_big
    pos = z >= 0.0
    one_m_lb = 1.0 - lb
    forget = lb + one_m_lb * jnp.where(pos, s_big, s_small)
    k = one_m_lb * jnp.where(pos, s_small, s_big)
    return q, forget, k


def _hgrn_prompt_kernel(q_ref, f_ref, i_ref, g_ref, lb_ref, ng_ref, yin_ref, y_ref, s_ref,
                        st_ref, lvl_ref, kpad, gpad, vpad, *, n_chunks):
    del yin_ref
    c = HGRN_CHUNK
    w = HGRN_BAND
    pad = HGRN_PAD
    nh = HGRN_HEADS_PER_STEP
    dk = HGRN_DK

    row = lax.broadcasted_iota(jnp.int32, (c, c), 0)
    col = lax.broadcasted_iota(jnp.int32, (c, c), 1)
    x = row ^ col
    lvl = jnp.where(x >= 64, 64, jnp.where(x >= 32, 32, jnp.where(x >= 16, 16, jnp.where(x >= 8, 8, 0))))
    lvl_ref[...] = jnp.where(row - col >= w, lvl, 0)
    st_ref[...] = jnp.zeros_like(st_ref)
    for ref in (kpad, gpad, vpad):
        ref[:, 0:pad, :] = jnp.zeros((nh, pad, dk), F32)
    r2 = lax.broadcasted_iota(jnp.int32, (2 * dk, 2 * dk), 0)
    c2 = lax.broadcasted_iota(jnp.int32, (2 * dk, 2 * dk), 1)
    pair_ones = jnp.where((r2 >= dk) == (c2 >= dk), 1.0, 0.0).astype(BF16)
    cur = pl.ds(pad, c)

    def head_chunk(hh, sl):
        cs = slice(hh * dk, (hh + 1) * dk)
        q, forget, k = _hgrn_gates(q_ref[sl, cs].astype(F32), f_ref[sl, cs].astype(F32),
                                   lb_ref[:, cs])
        v = i_ref[sl, cs].astype(F32)
        gcum = jnp.log2(forget)
        shift = 1
        while shift < c:
            gpad[hh, cur, :] = gcum
            gcum = gcum + gpad[hh, pl.ds(pad - shift, c), :]
            shift *= 2
        gpad[hh, cur, :] = gcum
        kpad[hh, cur, :] = k
        vpad[hh, cur, :] = v

        parts = []
        for d in range(w):
            if d == 0:
                p = q * k
            else:
                back = pl.ds(pad - d, c)
                p = q * kpad[hh, back, :] * jnp.exp2(gcum - gpad[hh, back, :])
            parts.append(p.astype(BF16))
        pairs = [jnp.concatenate(parts[2 * j:2 * j + 2], axis=1) for j in range(w // 2)]
        rs = jnp.dot(jnp.concatenate(pairs, axis=0), pair_ones, preferred_element_type=F32)
        o = rs[0:c, 0:dk] * v
        for d in range(1, w):
            j, half = divmod(d, 2)
            o = o + rs[j * c:(j + 1) * c, half * dk:(half + 1) * dk] * vpad[hh, pl.ds(pad - d, c), :]

        lv = lvl_ref[...]
        scores = jnp.zeros((c, c), F32)
        b = w
        while b < c:
            seg = gcum.reshape(c // (2 * b), 2 * b, dk)
            ref = jnp.broadcast_to(seg[:, b - 1:b, :], seg.shape).reshape(c, dk)
            a = jnp.exp2(-jnp.abs(gcum - ref))
            sc = lax.dot_general((q * a).astype(BF16), (k * a).astype(BF16),
                                 (((1,), (1,)), ((), ())), preferred_element_type=F32)
            scores = jnp.where(lv == b, sc, scores)
            b *= 2
        o = o + jnp.dot(scores.astype(BF16), v.astype(BF16), preferred_element_type=F32)

        st = st_ref[hh]
        qs = (q * jnp.exp2(gcum)).astype(BF16)
        o = o + lax.dot_general(qs, st.astype(BF16), (((1,), (1,)), ((), ())),
                                preferred_element_type=F32)
        glast = gcum[c - 1:c, :]
        ks = (k * jnp.exp2(glast - gcum)).astype(BF16)
        st_ref[hh] = st * jnp.exp2(glast) + jnp.dot(v.T.astype(BF16), ks, preferred_element_type=F32)

        ms = jnp.mean(o * o, axis=-1, keepdims=True)
        gate = ng_ref[:, cs] * _silu(g_ref[sl, cs].astype(F32))
        y_ref[sl, cs] = (o * lax.rsqrt(ms + LN_EPS) * gate).astype(y_ref.dtype)

    def chunk(ci, carry):
        sl = pl.ds(pl.multiple_of(ci * c, c), c)
        for hh in range(nh):
            head_chunk(hh, sl)
        return carry

    lax.fori_loop(0, n_chunks, chunk, 0)
    for hh in range(nh):
        s_ref[0, hh] = st_ref[hh].T


def _hgrn_prompt(proj, lb, ng, *, batch, seq, n_rows_out, col0):
    nh = HGRN_HEADS
    hs = HGRN_HEADS_PER_STEP
    wb = hs * HGRN_DK
    cb0 = col0 * HGRN_DK // wb
    nhb = nh // hs
    blk = lambda off: pl.BlockSpec((seq, wb), lambda b, h: (b, cb0 + off * nhb + h))
    vec = pl.BlockSpec((1, wb), lambda b, h: (0, h))
    kern = functools.partial(_hgrn_prompt_kernel, n_chunks=seq // HGRN_CHUNK)
    padded = pltpu.VMEM((hs, HGRN_PAD + HGRN_CHUNK, HGRN_DK), F32)
    return pl.pallas_call(
        kern,
        grid=(batch, nhb),
        in_specs=[blk(0), blk(1), blk(2), blk(3), vec, vec, pl.BlockSpec(memory_space=pl.ANY)],
        out_specs=[
            pl.BlockSpec((seq, wb), lambda b, h: (b, h)),
            pl.BlockSpec((1, hs, HGRN_DK, HGRN_DV), lambda b, h: (b, h, 0, 0)),
        ],
        out_shape=[
            jax.ShapeDtypeStruct((n_rows_out, nh * HGRN_DV), BF16),
            jax.ShapeDtypeStruct((batch, nh, HGRN_DK, HGRN_DV), F32),
        ],
        scratch_shapes=[
            pltpu.VMEM((hs, HGRN_DV, HGRN_DK), F32),
            pltpu.VMEM((HGRN_CHUNK, HGRN_CHUNK), jnp.int32),
            padded, padded, padded,
        ],
        input_output_aliases={6: 0},
        compiler_params=_params(32, 2),
        name="hgrn_prompt",
    )(proj, proj, proj, proj, lb, ng, jnp.zeros((n_rows_out, nh * HGRN_DV), BF16))


def _hgrn_sample_kernel(q_ref, f_ref, i_ref, g_ref, lb_ref, ng_ref, s_ref, yin_ref, sall_ref,
                        y_ref, so_ref, q_scr, f_scr, k_scr, v_scr, o_scr):
    del yin_ref, sall_ref
    nb = q_ref.shape[0]
    q, forget, k = _hgrn_gates(q_ref[...].astype(F32), f_ref[...].astype(F32), lb_ref[...])
    q_scr[...] = q
    f_scr[...] = forget
    k_scr[...] = k
    v_scr[...] = i_ref[...].astype(F32)
    sq = (HGRN_DK, HGRN_DV)

    def per_seq(n, carry):
        r = pl.ds(n, 1)
        qn, fn, kn, vn = q_scr[r, :], f_scr[r, :], k_scr[r, :], v_scr[r, :]
        for h in range(HGRN_HEADS):
            cs = slice(h * HGRN_DK, (h + 1) * HGRN_DK)
            col = lambda a: jnp.broadcast_to(a[:, cs], sq).T
            s_new = col(fn) * s_ref[n, h] + col(kn) * vn[:, h * HGRN_DV:(h + 1) * HGRN_DV]
            so_ref[n, h] = s_new
            o_scr[n, :, h * HGRN_DV:(h + 1) * HGRN_DV] = jnp.dot(
                qn[:, cs].astype(BF16), s_new.astype(BF16), preferred_element_type=F32)
        return carry

    lax.fori_loop(0, nb, per_seq, 0)
    o = jnp.concatenate([o_scr[n] for n in range(nb)], axis=0)
    gate = ng_ref[...] * _silu(g_ref[...].astype(F32))
    outs = []
    for h in range(HGRN_HEADS):
        oh = o[:, h * HGRN_DV:(h + 1) * HGRN_DV]
        ms = jnp.mean(oh * oh, axis=-1, keepdims=True)
        outs.append(oh * lax.rsqrt(ms + LN_EPS))
    y_ref[...] = (jnp.concatenate(outs, axis=-1) * gate).astype(y_ref.dtype)


def _hgrn_sample(proj, lb, ng, state, y_b, state_out, layer, *, n_prompt_rows, n_seq, col0):
    nb = SAMPLE_SEQ_TILE
    r0 = n_prompt_rows // nb
    wk = HGRN_HEADS * HGRN_DK
    cb0 = col0 * HGRN_DK // wk
    blk = lambda off: pl.BlockSpec((nb, wk), lambda i: (r0 + i, cb0 + off))
    vec = pl.BlockSpec((1, wk), lambda i: (0, 0))
    st_spec = pl.BlockSpec((None, nb, HGRN_HEADS, HGRN_DK, HGRN_DV), lambda i: (layer, i, 0, 0, 0))
    anywhere = pl.BlockSpec(memory_space=pl.ANY)
    return pl.pallas_call(
        _hgrn_sample_kernel,
        grid=(n_seq // nb,),
        in_specs=[blk(0), blk(1), blk(2), blk(3), vec, vec, st_spec, anywhere, anywhere],
        out_specs=[pl.BlockSpec((nb, wk), lambda i: (r0 + i, 0)), st_spec],
        out_shape=[
            jax.ShapeDtypeStruct(y_b.shape, y_b.dtype),
            jax.ShapeDtypeStruct(state_out.shape, state_out.dtype),
        ],
        scratch_shapes=[pltpu.VMEM((nb, wk), F32)] * 4 + [pltpu.VMEM((nb, 1, wk), F32)],
        input_output_aliases={7: 0, 8: 1},
        compiler_params=_params(48),
        name="hgrn_sample",
    )(proj, proj, proj, proj, lb, ng, state, y_b, state_out)


def _merge_kernel(ya_ref, yb_ref, ga_ref, gb_ref, x_ref, woa_ref, wob_ref, wo_ref,
                  g_ref, b_ref, rw_ref, rb_ref, x1_ref, logit_ref, z_scr, *, alpha):
    @pl.when(pl.program_id(0) == 0)
    def _():
        z_scr[...] = jnp.zeros_like(z_scr)

    x1 = _layer_norm(z_scr[...], g_ref[...], b_ref[...])
    x1_ref[...] = x1
    hi = x1.astype(BF16)
    mid = (x1 - hi.astype(F32)).astype(BF16)
    logit_ref[...] = jnp.dot(jnp.concatenate([hi, mid, hi], axis=1), rw_ref[...],
                             preferred_element_type=F32) + rb_ref[...]

    a = jnp.dot(ya_ref[...], woa_ref[...], preferred_element_type=F32)
    b = jnp.dot(yb_ref[...], wob_ref[...], preferred_element_type=F32)
    merged = (jax.nn.sigmoid(ga_ref[...].astype(F32)) * a
              + jax.nn.sigmoid(gb_ref[...].astype(F32)) * b)
    z_scr[...] = alpha * x_ref[...] + jnp.dot(merged.astype(BF16), wo_ref[...],
                                              preferred_element_type=F32)


def _merge(y_a, y_b, proj, x, woa, wob, wo, ln_g, ln_b, rw, rb, layer, *, alpha, gate_col_block):
    nt, d = x.shape
    wa = y_a.shape[1]
    tm = MERGE_ROW_TILE
    n_tiles = nt // tm
    const = lambda shape: pl.BlockSpec(shape, lambda i: (0,) * len(shape),
                                       pipeline_mode=pl.Buffered(1))
    stacked = lambda a: pl.BlockSpec((None,) + a.shape[1:], lambda i: (layer, 0, 0),
                                     pipeline_mode=pl.Buffered(1))
    cur = lambda col: (lambda i: (jnp.minimum(i, n_tiles - 1), col))
    prev = lambda i: (jnp.maximum(i - 1, 0), 0)
    return pl.pallas_call(
        functools.partial(_merge_kernel, alpha=alpha),
        grid=(n_tiles + 1,),
        in_specs=[
            pl.BlockSpec((tm, wa), cur(0)),
            pl.BlockSpec((tm, wa), cur(0)),
            pl.BlockSpec((tm, d), cur(gate_col_block)),
            pl.BlockSpec((tm, d), cur(gate_col_block + 1)),
            pl.BlockSpec((tm, d), cur(0)),
            stacked(woa), stacked(wob), stacked(wo),
            const(ln_g.shape), const(ln_b.shape), const(rw.shape), const(rb.shape),
        ],
        out_specs=[
            pl.BlockSpec((tm, d), prev),
            pl.BlockSpec((tm, V7X_LANES), prev),
        ],
        out_shape=[
            jax.ShapeDtypeStruct((nt, d), F32),
            jax.ShapeDtypeStruct((nt, V7X_LANES), F32),
        ],
        scratch_shapes=[pltpu.VMEM((tm, d), F32)],
        compiler_params=_params(56),
        name="merge_ln1",
    )(y_a, y_b, proj, proj, x, woa, wob, wo, ln_g, ln_b, rw, rb)


def _route_kernel(logit_ref, eid_ref, wcol_ref):
    tm = logit_ref.shape[0]
    lt = logit_ref[...].T[0:N_EXPERTS, :]
    e = jnp.exp(lt - jnp.max(lt, axis=0, keepdims=True))
    p = e / jnp.sum(e, axis=0, keepdims=True)
    pr = [p[j:j + 1, :] for j in range(N_EXPERTS)]

    best, gi = None, None
    for g in range(N_EXPERT_GROUPS):
        a, b, c, d = pr[4 * g:4 * g + 4]
        hi1, lo1 = jnp.maximum(a, b), jnp.minimum(a, b)
        hi2, lo2 = jnp.maximum(c, d), jnp.minimum(c, d)
        gs = jnp.maximum(hi1, hi2) + jnp.maximum(jnp.minimum(hi1, hi2), jnp.maximum(lo1, lo2))
        if g == 0:
            best, gi = gs, jnp.zeros((1, tm), jnp.int32)
        else:
            upd = gs > best
            best = jnp.where(upd, gs, best)
            gi = jnp.where(upd, g, gi)

    cand = []
    for j in range(EXPERTS_PER_GROUP):
        cj = pr[j]
        for g in range(1, N_EXPERT_GROUPS):
            cj = jnp.where(gi == g, pr[4 * g + j], cj)
        cand.append(cj)
    w1, i1 = cand[0], jnp.zeros((1, tm), jnp.int32)
    for j in range(1, EXPERTS_PER_GROUP):
        upd = cand[j] > w1
        w1 = jnp.where(upd, cand[j], w1)
        i1 = jnp.where(upd, j, i1)
    w2, i2 = jnp.full((1, tm), -1.0, F32), jnp.zeros((1, tm), jnp.int32)
    for j in range(EXPERTS_PER_GROUP):
        upd = (i1 != j) & (cand[j] > w2)
        w2 = jnp.where(upd, cand[j], w2)
        i2 = jnp.where(upd, j, i2)
    den = w1 + w2
    eid_ref[...] = jnp.concatenate([gi * EXPERTS_PER_GROUP + i1, gi * EXPERTS_PER_GROUP + i2], axis=0)
    wrows = jnp.concatenate([w1 / den, w2 / den, jnp.zeros((V7X_LANES - 2, tm), F32)], axis=0)
    wcol_ref[...] = wrows.T


def _route(logits):
    nt = logits.shape[0]
    tm = ROW_TILE
    return pl.pallas_call(
        _route_kernel,
        grid=(nt // tm,),
        in_specs=[pl.BlockSpec((tm, V7X_LANES), lambda i: (i, 0))],
        out_specs=[
            pl.BlockSpec((2, tm), lambda i: (0, i)),
            pl.BlockSpec((tm, V7X_LANES), lambda i: (i, 0)),
        ],
        out_shape=[
            jax.ShapeDtypeStruct((2, nt), jnp.int32),
            jax.ShapeDtypeStruct((nt, V7X_LANES), F32),
        ],
        compiler_params=_params(32),
        name="route",
    )(logits)


def _dispatch_plan(eid, *, n_rows_padded, tile):
    n_pairs = eid.size
    nt = eid.shape[1]
    e_flat = eid.reshape(n_pairs)
    onehot = (e_flat[:, None] == jnp.arange(N_EXPERTS, dtype=jnp.int32)[None, :]).astype(jnp.int32)
    csum = jnp.cumsum(onehot, axis=0)
    counts = csum[-1]
    rank = jnp.sum(onehot * csum, axis=1) - 1
    padded = ((counts + tile - 1) // tile) * tile
    ends = jnp.cumsum(padded)
    starts = ends - padded
    dest = starts[e_flat] + rank
    tok = jnp.arange(n_pairs, dtype=jnp.int32) % nt
    src = jnp.zeros((n_rows_padded,), jnp.int32).at[dest].set(tok, unique_indices=True)
    n_tiles = n_rows_padded // tile
    tile_start = jnp.arange(n_tiles, dtype=jnp.int32) * tile
    tile_expert = jnp.minimum(
        jnp.sum((tile_start[:, None] >= ends[None, :]).astype(jnp.int32), axis=1), N_EXPERTS - 1)
    n_valid = (ends[-1] // tile).astype(jnp.int32).reshape(1)
    last_expert = tile_expert[jnp.maximum(n_valid[0] - 1, 0)]
    tile_expert = jnp.where(jnp.arange(n_tiles) < n_valid[0], tile_expert, last_expert).astype(jnp.int32)
    return src, dest.reshape(2, nt).astype(jnp.int32), tile_expert, n_valid


def _row_copy(src_hbm, row, dst_vmem, r, sem):
    return pltpu.make_async_copy(src_hbm.at[pl.ds(row, 1), :], dst_vmem.at[pl.ds(r, 1), :], sem)


def _expert_kernel(te_ref, nvalid_ref, src_ref, src_next_ref, x_hbm, wg_ref, wu_ref, wd_ref, o_ref,
                   xbuf, sem):
    del te_ref
    t = pl.program_id(0)
    nv = nvalid_ref[0]
    tm = xbuf.shape[1]
    slot = lax.rem(t, 2)

    def issue(idx_ref, s):
        def body(g, carry):
            for j in range(DMA_UNROLL):
                r = g * DMA_UNROLL + j
                _row_copy(x_hbm, idx_ref[0, 0, r], xbuf.at[s], r, sem.at[s]).start(priority=j % 2)
            return carry
        lax.fori_loop(0, tm // DMA_UNROLL, body, 0)

    def drain(s):
        def body(g, carry):
            for j in range(DMA_UNROLL):
                _row_copy(x_hbm, 0, xbuf.at[s], g * DMA_UNROLL + j, sem.at[s]).wait()
            return carry
        lax.fori_loop(0, tm // DMA_UNROLL, body, 0)

    @pl.when((t == 0) & (nv > 0))
    def _():
        issue(src_ref, 0)

    @pl.when(t + 1 < nv)
    def _():
        issue(src_next_ref, 1 - slot)

    @pl.when(t < nv)
    def _():
        drain(slot)
        x = xbuf[slot].astype(BF16).astype(F32)
        g = jnp.dot(x, wg_ref[...], preferred_element_type=F32)
        u = jnp.dot(x, wu_ref[...], preferred_element_type=F32)
        h = (_silu(g) * u).astype(BF16).astype(F32)
        o_ref[...] = jnp.dot(h, wd_ref[...], preferred_element_type=F32)

    @pl.when(t >= nv)
    def _():
        o_ref[...] = jnp.zeros_like(o_ref)


def _experts(x1, src, w_gate, w_up, w_down, tile_expert, n_valid, layer, *, tile):
    d, dff = w_gate.shape[-2:]
    r = src.shape[0]
    n_tiles = r // tile
    src3 = src.reshape(n_tiles, 1, tile)
    wspec = lambda a, b, **kw: pl.BlockSpec((None, None, a, b),
                                            lambda t, te, nv: (layer, te[t], 0, 0), **kw)
    return pl.pallas_call(
        _expert_kernel,
        grid_spec=pltpu.PrefetchScalarGridSpec(
            num_scalar_prefetch=2,
            grid=(n_tiles,),
            in_specs=[
                pl.BlockSpec((1, 1, tile), lambda t, te, nv: (t, 0, 0), memory_space=pltpu.SMEM),
                pl.BlockSpec((1, 1, tile), lambda t, te, nv: (jnp.minimum(t + 1, n_tiles - 1), 0, 0),
                             memory_space=pltpu.SMEM),
                pl.BlockSpec(memory_space=pl.ANY),
                wspec(d, dff), wspec(d, dff),
                wspec(dff, d, pipeline_mode=pl.Buffered(1)),
            ],
            out_specs=pl.BlockSpec((tile, d), lambda t, te, nv: (t, 0)),
            scratch_shapes=[pltpu.VMEM((2, tile, d), F32), pltpu.SemaphoreType.DMA((2,))],
        ),
        out_shape=jax.ShapeDtypeStruct((r, d), F32),
        compiler_params=_params(58),
        name="moe_experts",
    )(tile_expert, n_valid, src3, src3, x1, w_gate, w_up, w_down)


def _combine_kernel(pos_ref, x_ref, wcol_ref, g_ref, b_ref, ys_hbm, x2_ref, x2b_ref,
                    buf0, buf1, sem, *, alpha):
    tm = x_ref.shape[0]
    part = tm // COMBINE_PARTS

    def copies(p, r, first_row, second_row):
        return (_row_copy(ys_hbm, first_row, buf0, r, sem.at[p]),
                _row_copy(ys_hbm, second_row, buf1, r, sem.at[p]))

    def issue(p):
        def body(g, carry):
            for j in range(DMA_UNROLL):
                r = p * part + g * DMA_UNROLL + j
                for cp in copies(p, r, pos_ref[0, 0, r], pos_ref[0, 1, r]):
                    cp.start(priority=j % 2)
            return carry
        lax.fori_loop(0, part // DMA_UNROLL, body, 0)

    def drain(p):
        def body(g, carry):
            for j in range(DMA_UNROLL):
                for cp in copies(p, p * part + g * DMA_UNROLL + j, 0, 0):
                    cp.wait()
            return carry
        lax.fori_loop(0, part // DMA_UNROLL, body, 0)

    for p in range(COMBINE_PARTS):
        issue(p)
    for p in range(COMBINE_PARTS):
        drain(p)
        rows = slice(p * part, (p + 1) * part)
        wc = wcol_ref[rows, :]
        z = alpha * x_ref[rows, :] + wc[:, 0:1] * buf0[rows, :] + wc[:, 1:2] * buf1[rows, :]
        x2 = _layer_norm(z, g_ref[...], b_ref[...])
        x2_ref[rows, :] = x2
        x2b_ref[rows, :] = x2.astype(BF16)


def _combine(pos, x1, wcol, ln_g, ln_b, ys, *, alpha):
    nt, d = x1.shape
    tm = COMBINE_ROW_TILE
    n_steps = nt // tm
    pos_t = pos.reshape(2, n_steps, tm).transpose(1, 0, 2)
    return pl.pallas_call(
        functools.partial(_combine_kernel, alpha=alpha),
        grid=(n_steps,),
        in_specs=[
            pl.BlockSpec((1, 2, tm), lambda i: (i, 0, 0), memory_space=pltpu.SMEM),
            pl.BlockSpec((tm, d), lambda i: (i, 0)),
            pl.BlockSpec((tm, V7X_LANES), lambda i: (i, 0)),
            pl.BlockSpec((1, d), lambda i: (0, 0)),
            pl.BlockSpec((1, d), lambda i: (0, 0)),
            pl.BlockSpec(memory_space=pl.ANY),
        ],
        out_specs=[pl.BlockSpec((tm, d), lambda i: (i, 0)), pl.BlockSpec((tm, d), lambda i: (i, 0))],
        out_shape=[jax.ShapeDtypeStruct((nt, d), F32), jax.ShapeDtypeStruct((nt, d), BF16)],
        scratch_shapes=[pltpu.VMEM((tm, d), F32), pltpu.VMEM((tm, d), F32),
                        pltpu.SemaphoreType.DMA((COMBINE_PARTS,))],
        compiler_params=_params(56),
        name="moe_combine_ln2",
    )(pos_t, x1, wcol, ln_g, ln_b, ys)


def kernel(x_prompt, x_sample, state_hgrn, w_in, sgu_ln_g, sgu_ln_b, w_s, b_s, lb_logits,
           hgrn_norm_g, w_out_a, w_out_b, w_o, ln1_g, ln1_b, router_w, router_b, w_gate, w_up,
           w_down, ln2_g, ln2_b):
    batch, seq, d = x_prompt.shape
    n_seq = x_sample.shape[0]
    depth = w_in.shape[0]
    a_width = sgu_ln_g.shape[1]
    n_prompt = batch * seq
    nt = n_prompt + n_seq
    alpha = (2.0 * depth) ** 0.25
    assert x_sample.shape[1] == 1 and n_seq == SGU_CHUNK and nt % ROW_TILE == 0
    assert a_width == SGU_GROUPS * SGU_CHUNK and d == 2 * a_width

    p = jax.nn.softmax(lb_logits.astype(F32), axis=0)
    lower_bounds = jnp.cumsum(p, axis=0) - p[0:1]

    n_rows_padded = 2 * nt + N_EXPERTS * MOE_ROW_TILE
    n_rows_padded = -(-n_rows_padded // MOE_ROW_TILE) * MOE_ROW_TILE

    woa_b, wob_b, wo_b = (a.astype(BF16) for a in (w_out_a, w_out_b, w_o))
    rw_pad = jnp.zeros((d, V7X_LANES), F32).at[:, :N_EXPERTS].set(router_w.astype(F32))
    rw_hi = rw_pad.astype(BF16)
    rw_mid = (rw_pad - rw_hi.astype(F32)).astype(BF16)
    rw3 = jnp.concatenate([rw_hi, rw_hi, rw_mid], axis=0)
    rb_pad = jnp.zeros((1, V7X_LANES), F32).at[0, :N_EXPERTS].set(router_b.astype(F32))
    s_sample = jnp.zeros(state_hgrn.shape, F32)

    causal = jnp.tril(jnp.ones((SGU_CHUNK, SGU_CHUNK), bool))
    eye = jnp.eye(SGU_CHUNK, dtype=F32)
    gd = a_width // SGU_GROUPS

    x = jnp.concatenate([x_prompt.reshape(n_prompt, d), x_sample.reshape(n_seq, d)], axis=0)
    x_b = x.astype(BF16)
    s_prompt, v_sample = [], []
    for l in range(depth):
        w_eff = jnp.stack([
            jnp.where(causal[None], w_s[l], 0.0),
            w_s[l][:, 0:1, 0:1] * eye[None],
        ]).astype(BF16)
        bias_eff = jnp.stack([
            jnp.repeat(b_s[l].T, gd, axis=1),
            jnp.broadcast_to(jnp.repeat(b_s[l][:, 0], gd)[None, :], (SGU_CHUNK, a_width)),
        ]).astype(F32)
        lb_l = lower_bounds[l].reshape(1, -1)
        ng_l = hgrn_norm_g[l].astype(F32).reshape(1, -1)

        proj = _matmul(x_b, w_in, l, tm=PROJ_ROW_TILE, tn=PROJ_COL_TILE, out_dtype=BF16)
        y_a, v_rows = _sgu(proj, w_eff, bias_eff, sgu_ln_g[l].reshape(1, -1).astype(F32),
                           sgu_ln_b[l].reshape(1, -1).astype(F32),
                           n_prompt_rows=n_prompt, width=a_width)
        q_col0 = 2 * a_width // HGRN_DK
        y_b, s_p = _hgrn_prompt(proj, lb_l, ng_l, batch=batch, seq=seq, n_rows_out=nt, col0=q_col0)
        y_b, s_sample = _hgrn_sample(proj, lb_l, ng_l, state_hgrn, y_b, s_sample, l,
                                     n_prompt_rows=n_prompt, n_seq=n_seq, col0=q_col0)
        gate_col_block = (2 * a_width + 4 * HGRN_HEADS * HGRN_DK) // d
        x1, logits = _merge(y_a, y_b, proj, x, woa_b, wob_b, wo_b,
                            ln1_g[l].reshape(1, -1).astype(F32), ln1_b[l].reshape(1, -1).astype(F32),
                            rw3, rb_pad, l, alpha=alpha, gate_col_block=gate_col_block)
        eid, wcol = _route(logits)
        src, pos, tile_expert, n_valid = _dispatch_plan(
            eid, n_rows_padded=n_rows_padded, tile=MOE_ROW_TILE)
        ys = _experts(x1, src, w_gate, w_up, w_down, tile_expert, n_valid, l, tile=MOE_ROW_TILE)
        x, x_b = _combine(pos, x1, wcol, ln2_g[l].reshape(1, -1).astype(F32),
                          ln2_b[l].reshape(1, -1).astype(F32), ys, alpha=alpha)
        s_prompt.append(s_p)
        v_sample.append(v_rows.reshape(n_seq, 1, a_width))

    y_prompt = x[:n_prompt].reshape(batch, seq, d)
    y_sample = x[n_prompt:].reshape(n_seq, 1, d)
    return (y_prompt, y_sample, jnp.stack(s_prompt), s_sample, jnp.stack(v_sample))
```
